```python
import math
import jax, jax.numpy as jnp
from jax import lax
import numpy as np

D_MODEL = 1024
BATCH = 32
SEQ = 256
DEPTH = 4
DEC_BATCH = 2
DEC_SEQ = 4096
PAST_LEN = 512

GRID_W = 64
N_MIXERS = 2
N_ATTN_LAYERS = (DEPTH + N_MIXERS - 1) // N_MIXERS
N_SSD_LAYERS = DEPTH // N_MIXERS
N_HEADS = 16
N_KV_HEADS = 4
HEAD_DIM = D_MODEL // N_HEADS
KV_REP = N_HEADS // N_KV_HEADS
ROPE_THETA = 10000.0
Q_BLOCK = 128
SSD_EXPAND = 2
D_INNER = SSD_EXPAND * D_MODEL
SSD_HEAD_DIM = 64
SSD_HEADS = D_INNER // SSD_HEAD_DIM
SSD_GROUPS = 4
SSD_REP = SSD_HEADS // SSD_GROUPS
D_STATE = 128
D_CONV = 5
CHUNK = 128
CONV_DIM = D_INNER + 2 * SSD_GROUPS * D_STATE
IN_PROJ_DIM = D_INNER + CONV_DIM + 2 * SSD_HEADS
N_EXPERTS = 16
N_EXPERT_GROUPS = 4
EXPERTS_PER_GROUP = N_EXPERTS // N_EXPERT_GROUPS
TOP_K = 2
D_FF_EXPERT = D_MODEL // 4
EPS = 1e-6

kernel_name = "hybrid_gqa_ssd_moe_diffusion_step"

F32 = jnp.float32


def rms_norm(x, g):
    xf = x.astype(F32)
    y = xf * lax.rsqrt(jnp.mean(xf * xf, axis=-1, keepdims=True) + EPS)
    return (y * g.astype(F32)).astype(x.dtype)


def modulation(cond, w, b):
    m = jax.nn.silu(cond) @ w + b
    return jnp.split(m, 6, axis=-1)


def adaln(x, g, shift, scale):
    return rms_norm(x, g) * (1 + scale) + shift


def axial_rope_tables(t_len):
    rows = t_len // GRID_W
    row_ids = jnp.repeat(jnp.arange(rows), GRID_W).astype(F32)
    col_ids = jnp.tile(jnp.arange(GRID_W), rows).astype(F32)
    n_freq = HEAD_DIM // 4
    inv = 1.0 / (ROPE_THETA ** (jnp.arange(n_freq, dtype=F32) / n_freq))
    ang = jnp.concatenate([row_ids[:, None] * inv, col_ids[:, None] * inv], axis=-1)
    return jnp.cos(ang), jnp.sin(ang)


def apply_rope(x, cos, sin):
    xf = x.astype(F32)
    x1, x2 = jnp.split(xf, 2, axis=-1)
    cs, sn = cos[None, :, None, :], sin[None, :, None, :]
    return jnp.concatenate([x1 * cs - x2 * sn, x2 * cs + x1 * sn], axis=-1).astype(x.dtype)


def blocked_attention(q, k, v):
    b, tq = q.shape[0], q.shape[1]
    nb = tq // Q_BLOCK
    qb = q.reshape(b, nb, Q_BLOCK, N_KV_HEADS, KV_REP, HEAD_DIM).transpose(1, 0, 2, 3, 4, 5)
    scale = HEAD_DIM ** -0.5

    def one_block(qblk):
        s = jnp.einsum('bqgrd,bkgd->bgrqk', qblk, k, preferred_element_type=F32) * scale
        p = jax.nn.softmax(s, axis=-1).astype(v.dtype)
        return jnp.einsum('bgrqk,bkgd->bqgrd', p, v)

    o = lax.map(one_block, qb)
    return o.transpose(1, 0, 2, 3, 4, 5).reshape(b, tq, N_HEADS * HEAD_DIM)


def qkv_project(h, w_qkv, q_g, k_g):
    b, t, _ = h.shape
    qkv = h @ w_qkv
    q, k, v = jnp.split(qkv, [N_HEADS * HEAD_DIM, (N_HEADS + N_KV_HEADS) * HEAD_DIM], axis=-1)
    q = rms_norm(q.reshape(b, t, N_HEADS, HEAD_DIM), q_g)
    k = rms_norm(k.reshape(b, t, N_KV_HEADS, HEAD_DIM), k_g)
    v = v.reshape(b, t, N_KV_HEADS, HEAD_DIM)
    return q, k, v


def attn_context(h, w_qkv, q_g, k_g, w_o):
    b, t, _ = h.shape
    q, k, v = qkv_project(h, w_qkv, q_g, k_g)
    o = blocked_attention(q.reshape(b, t, N_KV_HEADS, KV_REP, HEAD_DIM), k, v)
    return o @ w_o, k, v


def attn_latent(h, ctx_k, ctx_v, w_qkv, q_g, k_g, w_o):
    b, t, _ = h.shape
    q, k, v = qkv_project(h, w_qkv, q_g, k_g)
    cos, sin = axial_rope_tables(t)
    q = apply_rope(q, cos, sin)
    k = apply_rope(k, cos, sin)
    k_all = jnp.concatenate([k, ctx_k.astype(k.dtype)], axis=1)
    v_all = jnp.concatenate([v, ctx_v.astype(v.dtype)], axis=1)
    o = blocked_attention(q.reshape(b, t, N_KV_HEADS, KV_REP, HEAD_DIM), k_all, v_all)
    return o @ w_o


def dwconv_centred(u, w, bias):
    out = lax.conv_general_dilated(
        u, w[:, None, :].astype(u.dtype), window_strides=(1,),
        padding=[(D_CONV // 2, D_CONV // 2)],
        dimension_numbers=('NWC', 'WIO', 'NWC'),
        feature_group_count=u.shape[-1])
    return out + bias


def ssd_scan(x, dt, a, bm, cm, init):
    b, L = x.shape[0], x.shape[1]
    nc = L // CHUNK
    xs = (x * dt[..., None]).reshape(b, nc, CHUNK, SSD_GROUPS, SSD_REP, SSD_HEAD_DIM)
    da = (dt * a).reshape(b, nc, CHUNK, SSD_GROUPS, SSD_REP)
    bc = bm.reshape(b, nc, CHUNK, SSD_GROUPS, D_STATE)
    cc = cm.reshape(b, nc, CHUNK, SSD_GROUPS, D_STATE)
    acum = jnp.cumsum(da, axis=2)
    seg = acum[:, :, :, None] - acum[:, :, None, :]
    causal = jnp.tril(jnp.ones((CHUNK, CHUNK), dtype=bool))
    lmat = jnp.exp(jnp.where(causal[:, :, None, None], seg, -jnp.inf))
    cb = jnp.einsum('bcqgn,bcsgn->bcqsg', cc, bc)
    y_diag = jnp.einsum('bcqsgr,bcsgrp->bcqgrp', cb[..., None] * lmat, xs)
    decay = jnp.exp(acum[:, :, -1:] - acum)
    states = jnp.einsum('bcsgn,bcsgrp->bcgrpn', bc, xs * decay[..., None])
    chunk_decay = jnp.exp(acum[:, :, -1])

    def step(hstate, inp):
        st, dec = inp
        return hstate * dec[..., None, None] + st, hstate

    h0 = init.astype(F32).reshape(b, SSD_GROUPS, SSD_REP, SSD_HEAD_DIM, D_STATE)
    final, prev = lax.scan(step, h0, (jnp.moveaxis(states, 1, 0), jnp.moveaxis(chunk_decay, 1, 0)))
    prev = jnp.moveaxis(prev, 0, 1)
    y_off = jnp.einsum('bcqgn,bcgrpn->bcqgrp', cc, prev) * jnp.exp(acum)[..., None]
    y = (y_diag + y_off).reshape(b, L, SSD_HEADS, SSD_HEAD_DIM)
    return y, final.reshape(b, SSD_HEADS, SSD_HEAD_DIM, D_STATE)


def ssd_mixer(h, init_f, init_b, in_proj, conv_w, conv_b, dt_bias, a_log, d_skip, norm_g, out_proj):
    b, L, _ = h.shape
    zxbcdt = h @ in_proj
    z, xbc, dt = jnp.split(zxbcdt, [D_INNER, D_INNER + CONV_DIM], axis=-1)
    xbc = jax.nn.silu(dwconv_centred(xbc, conv_w, conv_b))
    x, bm, cm = jnp.split(xbc, [D_INNER, D_INNER + SSD_GROUPS * D_STATE], axis=-1)
    x = x.reshape(b, L, SSD_HEADS, SSD_HEAD_DIM).astype(F32)
    bm = bm.reshape(b, L, SSD_GROUPS, D_STATE).astype(F32)
    cm = cm.reshape(b, L, SSD_GROUPS, D_STATE).astype(F32)
    dt = jax.nn.softplus(dt.astype(F32).reshape(b, L, 2, SSD_HEADS) + dt_bias.astype(F32))
    a = -jnp.exp(a_log.astype(F32))
    flip = lambda t: jnp.flip(t, axis=1)
    y_f, s_f = ssd_scan(x, dt[:, :, 0], a[0], bm, cm, init_f)
    y_b, s_b = ssd_scan(flip(x), flip(dt[:, :, 1]), a[1], flip(bm), flip(cm), init_b)
    y = y_f + flip(y_b) + x * d_skip.astype(F32)[:, None]
    y = y.reshape(b, L, D_INNER).astype(h.dtype)
    y = rms_norm(y * jax.nn.silu(z), norm_g)
    return y @ out_proj, jnp.stack([s_f, s_b], axis=1)


def moe(h, router_w, router_b, w_gate, w_up, w_down):
    shp = h.shape
    t = h.reshape(-1, D_MODEL)
    scores = jax.nn.sigmoid((t @ router_w).astype(F32))
    biased = scores + router_b.astype(F32)
    grp = biased.reshape(-1, N_EXPERT_GROUPS, EXPERTS_PER_GROUP)
    grp_score = jnp.sum(lax.top_k(grp, TOP_K)[0], axis=-1)
    sel = jnp.argmax(grp_score, axis=-1)
    in_grp = (jnp.arange(N_EXPERTS) // EXPERTS_PER_GROUP)[None, :] == sel[:, None]
    _, idx = lax.top_k(jnp.where(in_grp, biased, -jnp.inf), TOP_K)
    w = jnp.take_along_axis(scores, idx, axis=-1)
    w = w / jnp.sum(w, axis=-1, keepdims=True)
    gates = jnp.sum(jax.nn.one_hot(idx, N_EXPERTS, dtype=F32) * w[..., None], axis=1)
    hg = jnp.einsum('td,edf->tef', t, w_gate)
    hu = jnp.einsum('td,edf->tef', t, w_up)
    act = jax.nn.silu(hg) * hu * gates[..., None].astype(t.dtype)
    out = jnp.einsum('tef,efd->td', act, w_down)
    return out.reshape(shp)


def setup_inputs(seed: int = 0) -> dict:
    key = jax.random.key(seed)
    ks = jax.random.split(key, 32)
    nrm = lambda k, shape, s: jax.random.normal(k, shape, F32) * s
    dt_min, dt_max = 0.001, 0.1
    u = jax.random.uniform(ks[20], (N_SSD_LAYERS, 2, SSD_HEADS), F32)
    dt0 = jnp.exp(u * (math.log(dt_max) - math.log(dt_min)) + math.log(dt_min))
    dt_bias = dt0 + jnp.log(-jnp.expm1(-dt0))
    a_log = jnp.log(jax.random.uniform(ks[21], (N_SSD_LAYERS, 2, SSD_HEADS), F32, 1.0, 16.0))
    return {
        "x_prompt": nrm(ks[0], (BATCH, SEQ, D_MODEL), 1.0),
        "x_sample": nrm(ks[1], (DEC_BATCH, DEC_SEQ, D_MODEL), 1.0),
        "c": nrm(ks[2], (DEC_BATCH, D_MODEL), 1.0),
        "c_ctx": nrm(ks[3], (D_MODEL,), 1.0),
        "cache_k": nrm(ks[4], (DEC_BATCH, N_ATTN_LAYERS, PAST_LEN, N_KV_HEADS, HEAD_DIM), 1.0),
        "cache_v": nrm(ks[5], (DEC_BATCH, N_ATTN_LAYERS, PAST_LEN, N_KV_HEADS, HEAD_DIM), 1.0),
        "state_ssm": nrm(ks[6], (DEC_BATCH, N_SSD_LAYERS, 2, SSD_HEADS, SSD_HEAD_DIM, D_STATE), 0.1),
        "mod_w": nrm(ks[7], (DEPTH, D_MODEL, 6 * D_MODEL), 0.5 * D_MODEL ** -0.5),
        "mod_b": nrm(ks[8], (DEPTH, 6 * D_MODEL), 0.01),
        "norm1_g": 1.0 + nrm(ks[9], (DEPTH, D_MODEL), 0.01),
        "norm2_g": 1.0 + nrm(ks[10], (DEPTH, D_MODEL), 0.01),
        "attn_w_qkv": nrm(ks[11], (N_ATTN_LAYERS, D_MODEL, (N_HEADS + 2 * N_KV_HEADS) * HEAD_DIM), D_MODEL ** -0.5),
        "attn_q_norm_g": 1.0 + nrm(ks[12], (N_ATTN_LAYERS, HEAD_DIM), 0.01),
        "attn_k_norm_g": 1.0 + nrm(ks[13], (N_ATTN_LAYERS, HEAD_DIM), 0.01),
        "attn_w_o": nrm(ks[14], (N_ATTN_LAYERS, N_HEADS * HEAD_DIM, D_MODEL), (N_HEADS * HEAD_DIM) ** -0.5),
        "ssd_in_proj": nrm(ks[15], (N_SSD_LAYERS, D_MODEL, IN_PROJ_DIM), D_MODEL ** -0.5),
        "ssd_conv_w": nrm(ks[16], (N_SSD_LAYERS, D_CONV, CONV_DIM), D_CONV ** -0.5),
        "ssd_conv_b": nrm(ks[17], (N_SSD_LAYERS, CONV_DIM), 0.01),
        "ssd_dt_bias": dt_bias,
        "ssd_a_log": a_log,
        "ssd_d": 1.0 + nrm(ks[18], (N_SSD_LAYERS, SSD_HEADS), 0.01),
        "ssd_norm_g": 1.0 + nrm(ks[19], (N_SSD_LAYERS, D_INNER), 0.01),
        "ssd_out_proj": nrm(ks[22], (N_SSD_LAYERS, D_INNER, D_MODEL), D_INNER ** -0.5),
        "router_w": nrm(ks[23], (D_MODEL, N_EXPERTS), D_MODEL ** -0.5),
        "router_b": nrm(ks[24], (N_EXPERTS,), 0.01),
        "expert_w_gate": nrm(ks[25], (DEPTH, N_EXPERTS, D_MODEL, D_FF_EXPERT), D_MODEL ** -0.5),
        "expert_w_up": nrm(ks[26], (DEPTH, N_EXPERTS, D_MODEL, D_FF_EXPERT), D_MODEL ** -0.5),
        "expert_w_down": nrm(ks[27], (DEPTH, N_EXPERTS, D_FF_EXPERT, D_MODEL), D_FF_EXPERT ** -0.5),
    }


def reference(x_prompt, x_sample, c, c_ctx, cache_k, cache_v, state_ssm,
              mod_w, mod_b, norm1_g, norm2_g,
              attn_w_qkv, attn_q_norm_g, attn_k_norm_g, attn_w_o,
              ssd_in_proj, ssd_conv_w, ssd_conv_b, ssd_dt_bias, ssd_a_log, ssd_d, ssd_norm_g, ssd_out_proj,
              router_w, router_b, expert_w_gate, expert_w_up, expert_w_down):
    xp, xs = x_prompt, x_sample
    cond_p = c_ctx[None, None, :]
    cond_s = c[:, None, :]
    zero_state = jnp.zeros((xp.shape[0], SSD_HEADS, SSD_HEAD_DIM, D_STATE), F32)
    new_k, new_v, new_ssm = [], [], []
    for layer in range(DEPTH):
        sh1p, sc1p, g1p, sh2p, sc2p, g2p = modulation(cond_p, mod_w[layer], mod_b[layer])
        sh1s, sc1s, g1s, sh2s, sc2s, g2s = modulation(cond_s, mod_w[layer], mod_b[layer])
        hp = adaln(xp, norm1_g[layer], sh1p, sc1p)
        hs = adaln(xs, norm1_g[layer], sh1s, sc1s)
        i = layer // N_MIXERS
        if layer % N_MIXERS == 0:
            yp, kp, vp = attn_context(hp, attn_w_qkv[i], attn_q_norm_g[i], attn_k_norm_g[i], attn_w_o[i])
            ys = attn_latent(hs, cache_k[:, i], cache_v[:, i], attn_w_qkv[i], attn_q_norm_g[i],
                             attn_k_norm_g[i], attn_w_o[i])
            new_k.append(kp)
            new_v.append(vp)
        else:
            yp, sp = ssd_mixer(hp, zero_state, zero_state, ssd_in_proj[i], ssd_conv_w[i], ssd_conv_b[i],
                               ssd_dt_bias[i], ssd_a_log[i], ssd_d[i], ssd_norm_g[i], ssd_out_proj[i])
            ys, _ = ssd_mixer(hs, state_ssm[:, i, 0], state_ssm[:, i, 1], ssd_in_proj[i], ssd_conv_w[i],
                              ssd_conv_b[i], ssd_dt_bias[i], ssd_a_log[i], ssd_d[i], ssd_norm_g[i],
                              ssd_out_proj[i])
            new_ssm.append(sp)
        xp = xp + g1p * yp
        xs = xs + g1s * ys
        xp = xp + g2p * moe(adaln(xp, norm2_g[layer], sh2p, sc2p), router_w, router_b,
                            expert_w_gate[layer], expert_w_up[layer], expert_w_down[layer])
        xs = xs + g2s * moe(adaln(xs, norm2_g[layer], sh2s, sc2s), router_w, router_b,
                            expert_w_gate[layer], expert_w_up[layer], expert_w_down[layer])
    new_cache_k = jnp.stack(new_k, axis=1)
    new_cache_v = jnp.stack(new_v, axis=1)
    new_state_ssm = jnp.stack(new_ssm, axis=1)
    return (xp, xs, new_cache_k, new_cache_v, new_state_ssm)
```

```python
import functools
import math

import jax
import jax.numpy as jnp
from jax import lax
from jax.experimental import pallas as pl
from jax.experimental.pallas import tpu as pltpu

F32 = jnp.float32
BF16 = jnp.bfloat16

D_MODEL = 1024
GRID_W = 64
N_HEADS = 16
N_KV_HEADS = 4
HEAD_DIM = 64
KV_REP = N_HEADS // N_KV_HEADS
ROPE_THETA = 10000.0
D_INNER = 2048
SSD_HEAD_DIM = 64
SSD_HEADS = 32
SSD_GROUPS = 4
SSD_REP = SSD_HEADS // SSD_GROUPS
D_STATE = 128
D_CONV = 5
CHUNK = 128
CONV_DIM = D_INNER + 2 * SSD_GROUPS * D_STATE
N_EXPERTS = 16
N_EXPERT_GROUPS = 4
EXPERTS_PER_GROUP = 4
D_FF_EXPERT = 256
EPS = 1e-6

LANES = 128
VMEM_LIMIT = 56 * 1024 * 1024


def _params(*sem):
    return pltpu.CompilerParams(dimension_semantics=sem, vmem_limit_bytes=VMEM_LIMIT)


def _silu(x):
    return x * jax.nn.sigmoid(x)


def _mod_kernel(c_ref, w_ref, b_ref, o_ref):
    s = _silu(c_ref[...])
    o_ref[...] = jnp.dot(s.astype(BF16), w_ref[...].astype(BF16),
                         preferred_element_type=F32) + b_ref[...]


def _modulation(cond8, mod_w, mod_b):
    depth, d, n = mod_w.shape
    tn = 1536
    return pl.pallas_call(
        _mod_kernel,
        grid=(depth, n // tn),
        in_specs=[pl.BlockSpec((8, d), lambda l, j: (0, 0)),
                  pl.BlockSpec((None, d, tn), lambda l, j: (l, 0, j)),
                  pl.BlockSpec((None, 1, tn), lambda l, j: (l, 0, j))],
        out_specs=pl.BlockSpec((None, 8, tn), lambda l, j: (l, 0, j)),
        out_shape=jax.ShapeDtypeStruct((depth, 8, n), F32),
        compiler_params=_params("parallel", "parallel"),
        name="modulation",
    )(cond8, mod_w, mod_b.reshape(depth, 1, n))


def _mod_spec(layer, chunk, tm, rows_per_mod):
    def imap(i, *_):
        return (layer, chunk, jnp.maximum(i * tm // rows_per_mod - 1, 0), 0, 0)
    return pl.BlockSpec((None, None, None, 1, D_MODEL), imap)


def _adaln(x, g, sh, sc):
    ms = jnp.mean(x * x, axis=-1, keepdims=True)
    y = x * lax.rsqrt(ms + EPS) * g
    return y * (1 + sc) + sh


def _ada_mm_kernel(x_ref, g_ref, sh_ref, sc_ref, w_ref, o_ref, h_ref):
    @pl.when(pl.program_id(1) == 0)
    def _():
        h_ref[...] = _adaln(x_ref[...], g_ref[...], sh_ref[...], sc_ref[...]).astype(BF16)

    o_ref[...] = jnp.dot(h_ref[...], w_ref[...], preferred_element_type=F32).astype(o_ref.dtype)


def _ada_mm(x, g, mod, layer, w, tn, out_dtype=F32, tm=512):
    t, d = x.shape
    n = w.shape[1]
    return pl.pallas_call(
        _ada_mm_kernel,
        grid=(t // tm, n // tn),
        in_specs=[pl.BlockSpec((tm, d), lambda i, j: (i, 0)),
                  pl.BlockSpec((1, d), lambda i, j: (0, 0)),
                  _mod_spec(layer, 0, tm, 4096),
                  _mod_spec(layer, 1, tm, 4096),
                  pl.BlockSpec((d, tn), lambda i, j: (0, j))],
        out_specs=pl.BlockSpec((tm, tn), lambda i, j: (i, j)),
        out_shape=jax.ShapeDtypeStruct((t, n), out_dtype),
        scratch_shapes=[pltpu.VMEM((tm, d), BF16)],
        compiler_params=_params("parallel", "arbitrary"),
        name="ada_mm",
    )(x, g, mod, mod, w)


def _mm_res_kernel(a_ref, w_ref, r_ref, gate_ref, o_ref):
    y = jnp.dot(a_ref[...], w_ref[...], preferred_element_type=F32)
    o_ref[...] = r_ref[...] + gate_ref[...] * y


def _norm_mm_res_kernel(a_ref, ng_ref, w_ref, r_ref, gate_ref, o_ref):
    a = a_ref[...]
    ms = jnp.mean(a * a, axis=-1, keepdims=True)
    an = (a * lax.rsqrt(ms + EPS) * ng_ref[...]).astype(BF16)
    y = jnp.dot(an, w_ref[...], preferred_element_type=F32)
    o_ref[...] = r_ref[...] + gate_ref[...] * y


def _mm_res(a, w, res, mod, layer, norm_g=None, tm=512):
    t, k = a.shape
    n = w.shape[1]
    a_spec = pl.BlockSpec((tm, k), lambda i: (i, 0))
    w_spec = pl.BlockSpec((k, n), lambda i: (0, 0))
    r_spec = pl.BlockSpec((tm, n), lambda i: (i, 0))
    gate_spec = _mod_spec(layer, 2, tm, 4096)
    if norm_g is None:
        kern, specs, args = _mm_res_kernel, [a_spec, w_spec, r_spec, gate_spec], (a, w, res, mod)
    else:
        kern = _norm_mm_res_kernel
        specs = [a_spec, pl.BlockSpec((1, k), lambda i: (0, 0)), w_spec, r_spec, gate_spec]
        args = (a, norm_g, w, res, mod)
    return pl.pallas_call(
        kern,
        grid=(t // tm,),
        in_specs=specs,
        out_specs=pl.BlockSpec((tm, n), lambda i: (i, 0)),
        out_shape=jax.ShapeDtypeStruct((t, n), F32),
        compiler_params=_params("parallel"),
        name="mm_res",
    )(*args)


def _lane_iota(rows):
    return lax.broadcasted_iota(jnp.int32, (rows, LANES), 1)


def _head_pair_norm(c, g2):
    lo = _lane_iota(c.shape[0]) < HEAD_DIM
    cc = c * c
    s_lo = jnp.sum(jnp.where(lo, cc, 0.0), axis=-1, keepdims=True)
    s_hi = jnp.sum(jnp.where(lo, 0.0, cc), axis=-1, keepdims=True)
    r = jnp.where(lo, lax.rsqrt(s_lo / HEAD_DIM + EPS), lax.rsqrt(s_hi / HEAD_DIM + EPS))
    return c * r * g2


def _rope_pair(y, cs, sn):
    first = (_lane_iota(y.shape[0]) % HEAD_DIM) < HEAD_DIM // 2
    partner = jnp.where(first, pltpu.roll(y, LANES - HEAD_DIM // 2, 1), pltpu.roll(y, HEAD_DIM // 2, 1))
    return y * cs + partner * sn


def _split_heads(c):
    lo = _lane_iota(c.shape[0]) < HEAD_DIM
    a_lo = jnp.where(lo, c, 0.0)
    b_hi = jnp.where(lo, 0.0, c)
    return a_lo, pltpu.roll(a_lo, HEAD_DIM, 1), pltpu.roll(b_hi, HEAD_DIM, 1), b_hi


def _store_split(ref, j, c):
    a_lo, a_hi, b_lo, b_hi = _split_heads(c)
    ref[2 * j, 0] = a_lo.astype(BF16)
    ref[2 * j, 1] = a_hi.astype(BF16)
    ref[2 * j + 1, 0] = b_lo.astype(BF16)
    ref[2 * j + 1, 1] = b_hi.astype(BF16)


def _qk_prep_kernel(*refs, rope, emit_cache):
    it = iter(refs)
    qkv_ref, qg_ref, kg_ref = next(it), next(it), next(it)
    cs_ref, sn_ref = (next(it), next(it)) if rope else (None, None)
    q_ref, k2_ref, v2_ref = next(it), next(it), next(it)
    kc_ref, vc_ref = (next(it), next(it)) if emit_cache else (None, None)

    nq = N_HEADS * HEAD_DIM
    nkv = N_KV_HEADS * HEAD_DIM
    scale = HEAD_DIM ** -0.5
    for j in range(nq // LANES):
        y = _head_pair_norm(qkv_ref[:, j * LANES:(j + 1) * LANES], qg_ref[...])
        if rope:
            y = _rope_pair(y, cs_ref[...], sn_ref[...])
        q_ref[:, j * LANES:(j + 1) * LANES] = (y * scale).astype(BF16)
    for j in range(nkv // LANES):
        y = _head_pair_norm(qkv_ref[:, nq + j * LANES:nq + (j + 1) * LANES], kg_ref[...])
        if rope:
            y = _rope_pair(y, cs_ref[...], sn_ref[...])
        if emit_cache:
            kc_ref[:, j * LANES:(j + 1) * LANES] = y
        _store_split(k2_ref, j, y)
        v = qkv_ref[:, nq + nkv + j * LANES:nq + nkv + (j + 1) * LANES]
        if emit_cache:
            vc_ref[:, j * LANES:(j + 1) * LANES] = v
        _store_split(v2_ref, j, v)


def _qk_prep(qkv, row0, rows, qg2, kg2, tables=None, emit_cache=False, tm=512):
    n = qkv.shape[1]
    nq = N_HEADS * HEAD_DIM
    nkv = N_KV_HEADS * HEAD_DIM
    off = row0 // tm
    rope = tables is not None
    in_specs = [pl.BlockSpec((tm, n), lambda i: (i + off, 0)),
                pl.BlockSpec((1, LANES), lambda i: (0, 0)),
                pl.BlockSpec((1, LANES), lambda i: (0, 0))]
    args = [qkv, qg2, kg2]
    if rope:
        nt = tables[0].shape[0] // tm
        in_specs += [pl.BlockSpec((tm, LANES), lambda i: (i % nt, 0))] * 2
        args += list(tables)
    kv_shape = jax.ShapeDtypeStruct((N_KV_HEADS, 2, rows, LANES), BF16)
    kv_spec = pl.BlockSpec((N_KV_HEADS, 2, tm, LANES), lambda i: (0, 0, i, 0))
    out_shape = [jax.ShapeDtypeStruct((rows, nq), BF16), kv_shape, kv_shape]
    out_specs = [pl.BlockSpec((tm, nq), lambda i: (i, 0)), kv_spec, kv_spec]
    if emit_cache:
        out_shape += [jax.ShapeDtypeStruct((rows, nkv), F32)] * 2
        out_specs += [pl.BlockSpec((tm, nkv), lambda i: (i, 0))] * 2
    return pl.pallas_call(
        functools.partial(_qk_prep_kernel, rope=rope, emit_cache=emit_cache),
        grid=(rows // tm,),
        in_specs=in_specs,
        out_specs=out_specs,
        out_shape=out_shape,
        compiler_params=_params("parallel"),
        name="qk_prep",
    )(*args)


def _kv_expand_kernel(k_ref, v_ref, k2_ref, v2_ref):
    for j in range(N_KV_HEADS * HEAD_DIM // LANES):
        _store_split(k2_ref, j, k_ref[:, j * LANES:(j + 1) * LANES])
        _store_split(v2_ref, j, v_ref[:, j * LANES:(j + 1) * LANES])


def _kv_expand(k, v, tm=512):
    rows, nkv = k.shape
    kv_shape = jax.ShapeDtypeStruct((N_KV_HEADS, 2, rows, LANES), BF16)
    kv_spec = pl.BlockSpec((N_KV_HEADS, 2, tm, LANES), lambda i: (0, 0, i, 0))
    return pl.pallas_call(
        _kv_expand_kernel,
        grid=(rows // tm,),
        in_specs=[pl.BlockSpec((tm, nkv), lambda i: (i, 0))] * 2,
        out_specs=[kv_spec, kv_spec],
        out_shape=[kv_shape, kv_shape],
        compiler_params=_params("parallel"),
        name="kv_expand",
    )(k, v)


def _attn_kernel(*refs, tk, n_tiles, has_ctx):
    if has_ctx:
        q_ref, k_ref, v_ref, kc_ref, vc_ref, o_ref = refs
    else:
        q_ref, k_ref, v_ref, o_ref = refs
        kc_ref = vc_ref = None
    tq = q_ref.shape[0]
    for pair in range(KV_REP // 2):
        qp = q_ref[:, pair * LANES:(pair + 1) * LANES]
        out = None
        for half in range(2):
            m = jnp.full((tq, 1), -jnp.inf, F32)
            l = jnp.zeros((tq, 1), F32)
            acc = jnp.zeros((tq, LANES), F32)
            tiles = [(k_ref, v_ref, t) for t in range(n_tiles)]
            if has_ctx:
                tiles += [(kc_ref, vc_ref, t) for t in range(kc_ref.shape[1] // tk)]
            for kr, vr, t in tiles:
                kt = kr[half, t * tk:(t + 1) * tk, :]
                vt = vr[half, t * tk:(t + 1) * tk, :]
                s = lax.dot_general(qp, kt, (((1,), (1,)), ((), ())), preferred_element_type=F32)
                m_new = jnp.maximum(m, jnp.max(s, axis=-1, keepdims=True))
                p = jnp.exp(s - m_new)
                alpha = jnp.exp(m - m_new)
                l = alpha * l + jnp.sum(p, axis=-1, keepdims=True)
                acc = alpha * acc + jnp.dot(p.astype(BF16), vt, preferred_element_type=F32)
                m = m_new
            o = acc / l
            out = o if out is None else out + o
        o_ref[:, pair * LANES:(pair + 1) * LANES] = out.astype(BF16)


def _attention(q, k2, v2, n_batch, t_len, ctx=None, tq=256, tk=512):
    tk = min(tk, t_len)
    nq = t_len // tq
    gw = KV_REP * HEAD_DIM
    q_spec = pl.BlockSpec((tq, gw), lambda b, g, i: (b * nq + i, g))
    kv_spec = pl.BlockSpec((None, 2, t_len, LANES), lambda b, g, i: (g, 0, b, 0))
    in_specs = [q_spec, kv_spec, kv_spec]
    args = [q, k2, v2]
    if ctx is not None:
        kc2, vc2, ctx_len = ctx
        c_spec = pl.BlockSpec((None, 2, ctx_len, LANES), lambda b, g, i: (g, 0, b, 0))
        in_specs += [c_spec, c_spec]
        args += [kc2, vc2]
    return pl.pallas_call(
        functools.partial(_attn_kernel, tk=tk, n_tiles=t_len // tk, has_ctx=ctx is not None),
        grid=(n_batch, N_KV_HEADS, nq),
        in_specs=in_specs,
        out_specs=q_spec,
        out_shape=jax.ShapeDtypeStruct(q.shape, BF16),
        compiler_params=_params("parallel", "parallel", "arbitrary"),
        name="attention",
    )(*args)


def _conv_kernel(u_ref, w_ref, b_ref, o_ref):
    u = u_ref[...]
    n = u.shape[0]
    row = lax.broadcasted_iota(jnp.int32, u.shape, 0)
    acc = u * w_ref[D_CONV // 2:D_CONV // 2 + 1, :] + b_ref[...]
    for k in range(D_CONV):
        d = k - D_CONV // 2
        if d == 0:
            continue
        shifted = pltpu.roll(u, (-d) % n, 0)
        valid = (row + d >= 0) & (row + d < n)
        acc = acc + jnp.where(valid, shifted, 0.0) * w_ref[k:k + 1, :]
    o_ref[...] = _silu(acc)


def _conv_silu(zx, col0, row0, n_seq, seq_len, w, b, tc):
    c = w.shape[1]
    coff, roff = col0 // tc, row0 // seq_len
    return pl.pallas_call(
        _conv_kernel,
        grid=(n_seq, c // tc),
        in_specs=[pl.BlockSpec((seq_len, tc), lambda s, j: (s + roff, j + coff)),
                  pl.BlockSpec((D_CONV, tc), lambda s, j: (0, j)),
                  pl.BlockSpec((1, tc), lambda s, j: (0, j))],
        out_specs=pl.BlockSpec((seq_len, tc), lambda s, j: (s, j)),
        out_shape=jax.ShapeDtypeStruct((n_seq * seq_len, c), F32),
        compiler_params=_params("parallel", "parallel"),
        name="conv_silu",
    )(zx, w, b)


def _cumsum_rows(v, reverse):
    n = v.shape[0]
    row = lax.broadcasted_iota(jnp.int32, v.shape, 0)
    k = 1
    while k < n:
        if reverse:
            v = v + jnp.where(row < n - k, pltpu.roll(v, n - k, 0), 0.0)
        else:
            v = v + jnp.where(row >= k, pltpu.roll(v, k, 0), 0.0)
        k *= 2
    return v


def _pair_cols(m, i):
    lo = _lane_iota(m.shape[0]) < SSD_HEAD_DIM
    return jnp.where(lo, m[:, i:i + 1], m[:, i + 1:i + 2])


def _ssd_kernel(*refs, rev, final_pass):
    it = iter(refs)
    x_ref, b_ref, c_ref, dt_ref, dtb_ref, alog_ref, init_ref = [next(it) for _ in range(7)]
    if final_pass:
        y0_ref, z_ref, dskip_ref = next(it), next(it), next(it)
    y_ref, fin_ref, st_ref = next(it), next(it), next(it)

    ci = pl.program_id(1)

    @pl.when(ci == 0)
    def _():
        st_ref[...] = init_ref[...]

    dirn = 1 if rev else 0
    dtv = jax.nn.softplus(dt_ref[...] + dtb_ref[...])
    da = dtv * (-jnp.exp(alog_ref[...]))
    acum = _cumsum_rows(da, rev)
    acum_t = acum.T
    last = 0 if rev else CHUNK - 1
    total = acum[last:last + 1, :]
    e_out = jnp.exp(acum)
    e_in = jnp.exp(total - acum)
    e_tot = jnp.exp(total)

    row = lax.broadcasted_iota(jnp.int32, (CHUNK, CHUNK), 0)
    col = lax.broadcasted_iota(jnp.int32, (CHUNK, CHUNK), 1)
    causal = (col >= row) if rev else (col <= row)
    lo = _lane_iota(CHUNK) < SSD_HEAD_DIM

    for g in range(SSD_GROUPS):
        bg = b_ref[:, g * D_STATE:(g + 1) * D_STATE].astype(BF16)
        cg = c_ref[:, g * D_STATE:(g + 1) * D_STATE].astype(BF16)
        cb = lax.dot_general(cg, bg, (((1,), (1,)), ((), ())), preferred_element_type=F32)
        for pp in range(SSD_REP // 2):
            h0 = g * SSD_REP + 2 * pp
            i0 = dirn * SSD_HEADS + h0
            cols = slice(h0 * SSD_HEAD_DIM, (h0 + 2) * SSD_HEAD_DIM)
            xp = x_ref[:, cols]
            xs = xp * _pair_cols(dtv, i0)
            xs_lo = jnp.where(lo, xs, 0.0).astype(BF16)
            xs_hi = jnp.where(lo, 0.0, xs).astype(BF16)
            y = None
            for hh, xh in ((0, xs_lo), (1, xs_hi)):
                a_col = acum[:, i0 + hh:i0 + hh + 1]
                a_row = acum_t[i0 + hh:i0 + hh + 1, :]
                lmat = jnp.exp(jnp.where(causal, a_col - a_row, -jnp.inf))
                yd = jnp.dot((cb * lmat).astype(BF16), xh, preferred_element_type=F32)
                y = yd if y is None else y + yd
            st_pair = st_ref[cols, :]
            y_off = lax.dot_general(cg, st_pair.astype(BF16), (((1,), (1,)), ((), ())),
                                    preferred_element_type=F32)
            y = y + y_off * _pair_cols(e_out, i0)
            xdec = (xs * _pair_cols(e_in, i0)).astype(BF16)
            new = lax.dot_general(xdec, bg, (((0,), (0,)), ((), ())), preferred_element_type=F32)
            half = lax.broadcasted_iota(jnp.int32, (2 * SSD_HEAD_DIM, D_STATE), 0) < SSD_HEAD_DIM
            dec = jnp.where(half, e_tot[:, i0:i0 + 1], e_tot[:, i0 + 1:i0 + 2])
            st_ref[cols, :] = st_pair * dec + new
            if final_pass:
                y = (y0_ref[:, cols] + y + xp * dskip_ref[:, cols]) * _silu(z_ref[:, cols])
            y_ref[:, cols] = y

    @pl.when(ci == pl.num_programs(1) - 1)
    def _():
        fin_ref[...] = st_ref[...]


def _ssd_scan(xbc, dt, dtb, alog, init, n_seq, seq_len, dt_row0, rev, final=None):
    nc = seq_len // CHUNK
    dirn = 1 if rev else 0
    dt_off = dt_row0 // CHUNK

    def tok(s, c):
        return s * nc + (nc - 1 - c if rev else c)

    nb = D_INNER // (SSD_GROUPS * D_STATE)
    in_specs = [pl.BlockSpec((CHUNK, D_INNER), lambda s, c: (tok(s, c), 0)),
                pl.BlockSpec((CHUNK, SSD_GROUPS * D_STATE), lambda s, c: (tok(s, c), nb)),
                pl.BlockSpec((CHUNK, SSD_GROUPS * D_STATE), lambda s, c: (tok(s, c), nb + 1)),
                pl.BlockSpec((CHUNK, LANES), lambda s, c: (tok(s, c) + dt_off, 0)),
                pl.BlockSpec((1, LANES), lambda s, c: (0, 0)),
                pl.BlockSpec((1, LANES), lambda s, c: (0, 0)),
                pl.BlockSpec((None, None, D_INNER, D_STATE), lambda s, c: (s, dirn, 0, 0))]
    args = [xbc, xbc, xbc, dt, dtb, alog, init]
    if final is not None:
        y0, zx, zx_row0, dskip = final
        z_off = zx_row0 // CHUNK
        in_specs += [pl.BlockSpec((CHUNK, D_INNER), lambda s, c: (tok(s, c), 0)),
                     pl.BlockSpec((CHUNK, D_INNER), lambda s, c: (tok(s, c) + z_off, 0)),
                     pl.BlockSpec((1, D_INNER), lambda s, c: (0, 0))]
        args += [y0, zx, dskip]
    return pl.pallas_call(
        functools.partial(_ssd_kernel, rev=rev, final_pass=final is not None),
        grid=(n_seq, nc),
        in_specs=in_specs,
        out_specs=[pl.BlockSpec((CHUNK, D_INNER), lambda s, c: (tok(s, c), 0)),
                   pl.BlockSpec((None, D_INNER, D_STATE), lambda s, c: (s, 0, 0))],
        out_shape=[jax.ShapeDtypeStruct((n_seq * seq_len, D_INNER), F32),
                   jax.ShapeDtypeStruct((n_seq, D_INNER, D_STATE), F32)],
        scratch_shapes=[pltpu.VMEM((D_INNER, D_STATE), F32)],
        compiler_params=_params("parallel", "arbitrary"),
        name="ssd_scan",
    )(*args)


def _route(biased_t, scores_t, gt_ref):
    ng, ne = N_EXPERT_GROUPS, EXPERTS_PER_GROUP
    b = [[biased_t[j * ne + k:j * ne + k + 1, :] for k in range(ne)] for j in range(ng)]
    s = [[scores_t[j * ne + k:j * ne + k + 1, :] for k in range(ne)] for j in range(ng)]
    gs = []
    for j in range(ng):
        best = None
        for k1 in range(ne):
            for k2 in range(k1 + 1, ne):
                ps = b[j][k1] + b[j][k2]
                best = ps if best is None else jnp.maximum(best, ps)
        gs.append(best)
    sel = jnp.zeros_like(gs[0], jnp.int32)
    best = gs[0]
    for j in range(1, ng):
        gt = gs[j] > best
        best = jnp.where(gt, gs[j], best)
        sel = jnp.where(gt, j, sel)
    in_grp = [sel == j for j in range(ng)]

    def pick(vals):
        out = []
        for k in range(ne):
            v = vals[ng - 1][k]
            for j in range(ng - 2, -1, -1):
                v = jnp.where(in_grp[j], vals[j][k], v)
            out.append(v)
        return out

    yb, ys = pick(b), pick(s)

    def argmax_first(vals):
        bv, bi = vals[0], jnp.zeros_like(sel)
        for k in range(1, ne):
            gt = vals[k] > bv
            bv = jnp.where(gt, vals[k], bv)
            bi = jnp.where(gt, k, bi)
        return bi

    i1 = argmax_first(yb)
    i2 = argmax_first([jnp.where(i1 == k, -jnp.inf, yb[k]) for k in range(ne)])

    def take(vals, idx):
        v = vals[ne - 1]
        for k in range(ne - 2, -1, -1):
            v = jnp.where(idx == k, vals[k], v)
        return v

    w1, w2 = take(ys, i1), take(ys, i2)
    tot = w1 + w2
    g1, g2 = w1 / tot, w2 / tot
    for j in range(ng):
        for k in range(ne):
            gate = jnp.where(in_grp[j] & (i1 == k), g1, 0.0) + jnp.where(in_grp[j] & (i2 == k), g2, 0.0)
            gt_ref[j * ne + k:j * ne + k + 1, :] = gate


def _moe_kernel(x_ref, g_ref, sh_ref, sc_ref, gate_ref, rw_ref, rb_ref, wg_ref, wu_ref, wd_ref,
                o_ref, h_ref, gates_ref, gt_ref, acc_ref):
    e = pl.program_id(1)

    @pl.when(e == 0)
    def _():
        h = _adaln(x_ref[...], g_ref[...], sh_ref[...], sc_ref[...]).astype(BF16)
        h_ref[...] = h
        logits_t = lax.dot_general(rw_ref[...], h, (((1,), (1,)), ((), ())),
                                   preferred_element_type=F32)
        scores_t = jax.nn.sigmoid(logits_t)
        gt_ref[...] = jnp.zeros_like(gt_ref)
        _route(scores_t + rb_ref[...], scores_t, gt_ref)
        gates_ref[...] = gt_ref[...].T
        acc_ref[...] = jnp.zeros_like(acc_ref)

    h = h_ref[...]
    hg = jnp.dot(h, wg_ref[...], preferred_element_type=F32)
    hu = jnp.dot(h, wu_ref[...], preferred_element_type=F32)
    lane = _lane_iota(h.shape[0])
    gate = jnp.sum(jnp.where(lane == e, gates_ref[...], 0.0), axis=-1, keepdims=True)
    act = (_silu(hg) * hu * gate).astype(BF16)
    acc_ref[...] += jnp.dot(act, wd_ref[...], preferred_element_type=F32)

    @pl.when(e == pl.num_programs(1) - 1)
    def _():
        o_ref[...] = x_ref[...] + gate_ref[...] * acc_ref[...]


def _moe(x, g, mod, layer, rw_t, rb_col, wg, wu, wd, tm=1024):
    t, d = x.shape
    ne, _, f = wg.shape
    return pl.pallas_call(
        _moe_kernel,
        grid=(t // tm, ne),
        in_specs=[pl.BlockSpec((tm, d), lambda i, e: (i, 0)),
                  pl.BlockSpec((1, d), lambda i, e: (0, 0)),
                  _mod_spec(layer, 3, tm, 4096),
                  _mod_spec(layer, 4, tm, 4096),
                  _mod_spec(layer, 5, tm, 4096),
                  pl.BlockSpec((LANES, d), lambda i, e: (0, 0)),
                  pl.BlockSpec((LANES, 1), lambda i, e: (0, 0)),
                  pl.BlockSpec((None, d, f), lambda i, e: (e, 0, 0)),
                  pl.BlockSpec((None, d, f), lambda i, e: (e, 0, 0)),
                  pl.BlockSpec((None, f, d), lambda i, e: (e, 0, 0))],
        out_specs=pl.BlockSpec((tm, d), lambda i, e: (i, 0)),
        out_shape=jax.ShapeDtypeStruct((t, d), F32),
        scratch_shapes=[pltpu.VMEM((tm, d), BF16), pltpu.VMEM((tm, LANES), F32),
                        pltpu.VMEM((LANES, tm), F32), pltpu.VMEM((tm, d), F32)],
        compiler_params=_params("parallel", "arbitrary"),
        name="moe",
    )(x, g, mod, mod, mod, rw_t, rb_col, wg, wu, wd)


def _rope_tables(t_len):
    rows = t_len // GRID_W
    row_ids = jnp.repeat(jnp.arange(rows), GRID_W).astype(F32)
    col_ids = jnp.tile(jnp.arange(GRID_W), rows).astype(F32)
    n_freq = HEAD_DIM // 4
    inv = 1.0 / (ROPE_THETA ** (jnp.arange(n_freq, dtype=F32) / n_freq))
    ang = jnp.concatenate([row_ids[:, None] * inv, col_ids[:, None] * inv], axis=-1)
    cos, sin = jnp.cos(ang), jnp.sin(ang)
    cs = jnp.tile(jnp.concatenate([cos, cos], axis=-1), (1, LANES // HEAD_DIM))
    sn = jnp.tile(jnp.concatenate([-sin, sin], axis=-1), (1, LANES // HEAD_DIM))
    return cs, sn


def _pad_lanes(v):
    return jnp.pad(v.reshape(1, -1), ((0, 0), (0, LANES - v.size)))


def kernel(x_prompt, x_sample, c, c_ctx, cache_k, cache_v, state_ssm, mod_w, mod_b, norm1_g, norm2_g,
           attn_w_qkv, attn_q_norm_g, attn_k_norm_g, attn_w_o, ssd_in_proj, ssd_conv_w, ssd_conv_b,
           ssd_dt_bias, ssd_a_log, ssd_d, ssd_norm_g, ssd_out_proj, router_w, router_b, expert_w_gate,
           expert_w_up, expert_w_down):
    n_p, l_p, d = x_prompt.shape
    n_s, l_s, _ = x_sample.shape
    t_p, t_s = n_p * l_p, n_s * l_s
    depth = mod_w.shape[0]
    past = cache_k.shape[2]
    nkv = N_KV_HEADS * HEAD_DIM
    assert t_p == 2 * l_s and n_s == 2 and d == D_MODEL

    x = jnp.concatenate([x_prompt.reshape(t_p, d), x_sample.reshape(t_s, d)], axis=0)
    cond8 = jnp.concatenate([c_ctx[None], c, jnp.zeros((8 - 1 - n_s, d), F32)], axis=0)
    mod = _modulation(cond8, mod_w, mod_b)
    mod = mod[:, :1 + n_s].reshape(depth, 1 + n_s, 6, 1, d).transpose(0, 2, 1, 3, 4)

    rope = _rope_tables(l_s)
    rw_t = jnp.pad(router_w.T, ((0, LANES - N_EXPERTS), (0, 0))).astype(BF16)
    rb_col = jnp.pad(router_b, (0, LANES - N_EXPERTS)).reshape(LANES, 1)
    wg_all, wu_all, wd_all = (w.astype(BF16) for w in (expert_w_gate, expert_w_up, expert_w_down))

    new_k, new_v, new_ssm = [], [], []
    for layer in range(depth):
        i = layer // 2
        g1 = norm1_g[layer][None]
        if layer % 2 == 0:
            qkv = _ada_mm(x, g1, mod, layer, attn_w_qkv[i].astype(BF16), tn=attn_w_qkv.shape[2])
            qg2 = jnp.tile(attn_q_norm_g[i], LANES // HEAD_DIM)[None]
            kg2 = jnp.tile(attn_k_norm_g[i], LANES // HEAD_DIM)[None]
            q_p, k2_p, v2_p, kc, vc = _qk_prep(qkv, 0, t_p, qg2, kg2, emit_cache=True)
            q_s, k2_s, v2_s = _qk_prep(qkv, t_p, t_s, qg2, kg2, tables=rope)
            kc2, vc2 = _kv_expand(cache_k[:, i].reshape(n_s * past, nkv), cache_v[:, i].reshape(n_s * past, nkv))
            o_p = _attention(q_p, k2_p, v2_p, n_p, l_p)
            o_s = _attention(q_s, k2_s, v2_s, n_s, l_s, ctx=(kc2, vc2, past))
            o = jnp.concatenate([o_p, o_s], axis=0)
            x = _mm_res(o, attn_w_o[i].astype(BF16), x, mod, layer)
            new_k.append(kc.reshape(n_p, l_p, N_KV_HEADS, HEAD_DIM))
            new_v.append(vc.reshape(n_p, l_p, N_KV_HEADS, HEAD_DIM))
        else:
            w_in = ssd_in_proj[i]
            nzx = D_INNER + CONV_DIM
            zx = _ada_mm(x, g1, mod, layer, w_in[:, :nzx].astype(BF16), tn=1024)
            w_dt = jnp.pad(w_in[:, nzx:], ((0, 0), (0, LANES - 2 * SSD_HEADS))).astype(BF16)
            dt = _ada_mm(x, g1, mod, layer, w_dt, tn=LANES)
            dtb = _pad_lanes(ssd_dt_bias[i])
            alog = _pad_lanes(ssd_a_log[i])
            dskip = jnp.repeat(ssd_d[i], SSD_HEAD_DIM)[None]
            cw, cb = ssd_conv_w[i], ssd_conv_b[i][None]
            ys = []
            for row0, n_seq, seq_len, init, tc in (
                    (0, n_p, l_p, jnp.zeros((n_p, 2, D_INNER, D_STATE), F32), 1024),
                    (t_p, n_s, l_s, state_ssm[:, i].reshape(n_s, 2, D_INNER, D_STATE), 256)):
                xbc = _conv_silu(zx, D_INNER, row0, n_seq, seq_len, cw, cb, tc)
                y_f, s_f = _ssd_scan(xbc, dt, dtb, alog, init, n_seq, seq_len, row0, rev=False)
                y, s_b = _ssd_scan(xbc, dt, dtb, alog, init, n_seq, seq_len, row0, rev=True,
                                   final=(y_f, zx, row0, dskip))
                ys.append(y)
                if row0 == 0:
                    new_ssm.append(jnp.stack([s_f, s_b], axis=1).reshape(
                        n_p, 2, SSD_HEADS, SSD_HEAD_DIM, D_STATE))
            yg = jnp.concatenate(ys, axis=0)
            x = _mm_res(yg, ssd_out_proj[i].astype(BF16), x, mod, layer, norm_g=ssd_norm_g[i][None])
        x = _moe(x, norm2_g[layer][None], mod, layer, rw_t, rb_col,
                 wg_all[layer], wu_all[layer], wd_all[layer])

    y_prompt = x[:t_p].reshape(n_p, l_p, d)
    y_sample = x[t_p:].reshape(n_s, l_s, d)
    return (y_prompt, y_sample, jnp.stack(new_k, axis=1), jnp.stack(new_v, axis=1),
            jnp.stack(new_ssm, axis=1))
```

```python
import functools
import math

import jax
import jax.numpy as jnp
from jax import lax
from jax.experimental import pallas as pl
from jax.experimental.pallas import tpu as pltpu

F32 = jnp.float32
BF16 = jnp.bfloat16

D_MODEL = 1024
GRID_W = 64
N_HEADS = 16
N_KV_HEADS = 4
HEAD_DIM = 64
KV_REP = N_HEADS // N_KV_HEADS
ROPE_THETA = 10000.0
D_INNER = 2048
SSD_HEAD_DIM = 64
SSD_HEADS = 32
SSD_GROUPS = 4
SSD_REP = SSD_HEADS // SSD_GROUPS
D_STATE = 128
D_CONV = 5
CHUNK = 128
CONV_DIM = D_INNER + 2 * SSD_GROUPS * D_STATE
N_EXPERTS = 16
N_EXPERT_GROUPS = 4
EXPERTS_PER_GROUP = 4
D_FF_EXPERT = 256
EPS = 1e-6

LANES = 128
VMEM_LIMIT = 56 * 1024 * 1024


def _params(*sem):
    return pltpu.CompilerParams(dimension_semantics=sem, vmem_limit_bytes=VMEM_LIMIT)


def _silu(x):
    return x * jax.nn.sigmoid(x)


def _mod_kernel(c_ref, w_ref, b_ref, o_ref):
    s = _silu(c_ref[...])
    o_ref[...] = jnp.dot(s.astype(BF16), w_ref[...].astype(BF16),
                         preferred_element_type=F32) + b_ref[...]


def _modulation(cond8, mod_w, mod_b):
    depth, d, n = mod_w.shape
    tn = 1536
    return pl.pallas_call(
        _mod_kernel,
        grid=(depth, n // tn),
        in_specs=[pl.BlockSpec((8, d), lambda l, j: (0, 0)),
                  pl.BlockSpec((None, d, tn), lambda l, j: (l, 0, j)),
                  pl.BlockSpec((None, 1, tn), lambda l, j: (l, 0, j))],
        out_specs=pl.BlockSpec((None, 8, tn), lambda l, j: (l, 0, j)),
        out_shape=jax.ShapeDtypeStruct((depth, 8, n), F32),
        compiler_params=_params("parallel", "parallel"),
        name="modulation",
    )(cond8, mod_w, mod_b.reshape(depth, 1, n))


def _mod_spec(layer, chunk, tm, rows_per_mod):
    def imap(i, *_):
        return (layer, chunk, jnp.maximum(i * tm // rows_per_mod - 1, 0), 0, 0)
    return pl.BlockSpec((None, None, None, 1, D_MODEL), imap)


def _adaln(x, g, sh, sc):
    ms = jnp.mean(x * x, axis=-1, keepdims=True)
    y = x * lax.rsqrt(ms + EPS) * g
    return y * (1 + sc) + sh


def _ada_mm_kernel(x_ref, g_ref, sh_ref, sc_ref, w_ref, o_ref, h_ref):
    @pl.when(pl.program_id(1) == 0)
    def _():
        h_ref[...] = _adaln(x_ref[...], g_ref[...], sh_ref[...], sc_ref[...]).astype(BF16)

    o_ref[...] = jnp.dot(h_ref[...], w_ref[...], preferred_element_type=F32).astype(o_ref.dtype)


def _ada_mm(x, g, mod, layer, w, tn, out_dtype=F32, tm=512):
    t, d = x.shape
    n = w.shape[1]
    return pl.pallas_call(
        _ada_mm_kernel,
        grid=(t // tm, n // tn),
        in_specs=[pl.BlockSpec((tm, d), lambda i, j: (i, 0)),
                  pl.BlockSpec((1, d), lambda i, j: (0, 0)),
                  _mod_spec(layer, 0, tm, 4096),
                  _mod_spec(layer, 1, tm, 4096),
                  pl.BlockSpec((d, tn), lambda i, j: (0, j))],
        out_specs=pl.BlockSpec((tm, tn), lambda i, j: (i, j)),
        out_shape=jax.ShapeDtypeStruct((t, n), out_dtype),
        scratch_shapes=[pltpu.VMEM((tm, d), BF16)],
        compiler_params=_params("parallel", "arbitrary"),
        name="ada_mm",
    )(x, g, mod, mod, w)


def _mm_res_kernel(a_ref, w_ref, r_ref, gate_ref, o_ref):
    y = jnp.dot(a_ref[...], w_ref[...], preferred_element_type=F32)
    o_ref[...] = r_ref[...] + gate_ref[...] * y


def _norm_mm_res_kernel(a_ref, ng_ref, w_ref, r_ref, gate_ref, o_ref):
    a = a_ref[...]
    ms = jnp.mean(a * a, axis=-1, keepdims=True)
    an = (a * lax.rsqrt(ms + EPS) * ng_ref[...]).astype(BF16)
    y = jnp.dot(an, w_ref[...], preferred_element_type=F32)
    o_ref[...] = r_ref[...] + gate_ref[...] * y


def _mm_res(a, w, res, mod, layer, norm_g=None, tm=512):
    t, k = a.shape
    n = w.shape[1]
    a_spec = pl.BlockSpec((tm, k), lambda i: (i, 0))
    w_spec = pl.BlockSpec((k, n), lambda i: (0, 0))
    r_spec = pl.BlockSpec((tm, n), lambda i: (i, 0))
    gate_spec = _mod_spec(layer, 2, tm, 4096)
    if norm_g is None:
        kern, specs, args = _mm_res_kernel, [a_spec, w_spec, r_spec, gate_spec], (a, w, res, mod)
    else:
        kern = _norm_mm_res_kernel
        specs = [a_spec, pl.BlockSpec((1, k), lambda i: (0, 0)), w_spec, r_spec, gate_spec]
        args = (a, norm_g, w, res, mod)
    return pl.pallas_call(
        kern,
        grid=(t // tm,),
        in_specs=specs,
        out_specs=pl.BlockSpec((tm, n), lambda i: (i, 0)),
        out_shape=jax.ShapeDtypeStruct((t, n), F32),
        compiler_params=_params("parallel"),
        name="mm_res",
    )(*args)


def _lane_iota(rows):
    return lax.broadcasted_iota(jnp.int32, (rows, LANES), 1)


def _head_pair_norm(c, g2):
    lo = _lane_iota(c.shape[0]) < HEAD_DIM
    cc = c * c
    s_lo = jnp.sum(jnp.where(lo, cc, 0.0), axis=-1, keepdims=True)
    s_hi = jnp.sum(jnp.where(lo, 0.0, cc), axis=-1, keepdims=True)
    r = jnp.where(lo, lax.rsqrt(s_lo / HEAD_DIM + EPS), lax.rsqrt(s_hi / HEAD_DIM + EPS))
    return c * r * g2


def _rope_pair(y, cs, sn):
    first = (_lane_iota(y.shape[0]) % HEAD_DIM) < HEAD_DIM // 2
    partner = jnp.where(first, pltpu.roll(y, LANES - HEAD_DIM // 2, 1), pltpu.roll(y, HEAD_DIM // 2, 1))
    return y * cs + partner * sn


def _store_v_split(ref, j, c):
    lane = _lane_iota(c.shape[0])
    lo = lane < HEAD_DIM
    a_lo = jnp.where(lo, c, jnp.where(lane == HEAD_DIM, 1.0, 0.0))
    b_hi = jnp.where(lo, jnp.where(lane == 0, 1.0, 0.0), c)
    ref[2 * j, 0] = a_lo.astype(BF16)
    ref[2 * j, 1] = pltpu.roll(a_lo, HEAD_DIM, 1).astype(BF16)
    ref[2 * j + 1, 0] = pltpu.roll(b_hi, HEAD_DIM, 1).astype(BF16)
    ref[2 * j + 1, 1] = b_hi.astype(BF16)


def _store_kt_split(ref, j, c):
    t = c.T
    a, b = t[:HEAD_DIM], t[HEAD_DIM:]
    z = jnp.zeros_like(a)
    ref[2 * j, 0] = jnp.concatenate([a, z], axis=0).astype(BF16)
    ref[2 * j, 1] = jnp.concatenate([z, a], axis=0).astype(BF16)
    ref[2 * j + 1, 0] = jnp.concatenate([b, z], axis=0).astype(BF16)
    ref[2 * j + 1, 1] = jnp.concatenate([z, b], axis=0).astype(BF16)


def _qk_prep_kernel(*refs, rope, emit_cache):
    it = iter(refs)
    qkv_ref, qg_ref, kg_ref = next(it), next(it), next(it)
    cs_ref, sn_ref = (next(it), next(it)) if rope else (None, None)
    q_ref, k2_ref, v2_ref = next(it), next(it), next(it)
    kc_ref, vc_ref = (next(it), next(it)) if emit_cache else (None, None)

    nq = N_HEADS * HEAD_DIM
    nkv = N_KV_HEADS * HEAD_DIM
    scale = HEAD_DIM ** -0.5 * math.log2(math.e)
    for j in range(nq // LANES):
        y = _head_pair_norm(qkv_ref[:, j * LANES:(j + 1) * LANES], qg_ref[...])
        if rope:
            y = _rope_pair(y, cs_ref[...], sn_ref[...])
        q_ref[:, j * LANES:(j + 1) * LANES] = (y * scale).astype(BF16)
    for j in range(nkv // LANES):
        y = _head_pair_norm(qkv_ref[:, nq + j * LANES:nq + (j + 1) * LANES], kg_ref[...])
        if rope:
            y = _rope_pair(y, cs_ref[...], sn_ref[...])
        if emit_cache:
            kc_ref[:, j * LANES:(j + 1) * LANES] = y
        _store_kt_split(k2_ref, j, y)
        v = qkv_ref[:, nq + nkv + j * LANES:nq + nkv + (j + 1) * LANES]
        if emit_cache:
            vc_ref[:, j * LANES:(j + 1) * LANES] = v
        _store_v_split(v2_ref, j, v)


def _kv_shapes(rows):
    return (jax.ShapeDtypeStruct((N_KV_HEADS, 2, LANES, rows), BF16),
            jax.ShapeDtypeStruct((N_KV_HEADS, 2, rows, LANES), BF16))


def _kv_specs(tm):
    return (pl.BlockSpec((N_KV_HEADS, 2, LANES, tm), lambda i: (0, 0, 0, i)),
            pl.BlockSpec((N_KV_HEADS, 2, tm, LANES), lambda i: (0, 0, i, 0)))


def _qk_prep(qkv, row0, rows, qg2, kg2, tables=None, emit_cache=False, tm=512):
    n = qkv.shape[1]
    nq = N_HEADS * HEAD_DIM
    nkv = N_KV_HEADS * HEAD_DIM
    off = row0 // tm
    rope = tables is not None
    in_specs = [pl.BlockSpec((tm, n), lambda i: (i + off, 0)),
                pl.BlockSpec((1, LANES), lambda i: (0, 0)),
                pl.BlockSpec((1, LANES), lambda i: (0, 0))]
    args = [qkv, qg2, kg2]
    if rope:
        nt = tables[0].shape[0] // tm
        in_specs += [pl.BlockSpec((tm, LANES), lambda i: (i % nt, 0))] * 2
        args += list(tables)
    out_shape = [jax.ShapeDtypeStruct((rows, nq), BF16), *_kv_shapes(rows)]
    out_specs = [pl.BlockSpec((tm, nq), lambda i: (i, 0)), *_kv_specs(tm)]
    if emit_cache:
        out_shape += [jax.ShapeDtypeStruct((rows, nkv), F32)] * 2
        out_specs += [pl.BlockSpec((tm, nkv), lambda i: (i, 0))] * 2
    return pl.pallas_call(
        functools.partial(_qk_prep_kernel, rope=rope, emit_cache=emit_cache),
        grid=(rows // tm,),
        in_specs=in_specs,
        out_specs=out_specs,
        out_shape=out_shape,
        compiler_params=_params("parallel"),
        name="qk_prep",
    )(*args)


def _kv_expand_kernel(k_ref, v_ref, k2_ref, v2_ref):
    for j in range(N_KV_HEADS * HEAD_DIM // LANES):
        _store_kt_split(k2_ref, j, k_ref[:, j * LANES:(j + 1) * LANES])
        _store_v_split(v2_ref, j, v_ref[:, j * LANES:(j + 1) * LANES])


def _kv_expand(k, v, tm=512):
    rows, nkv = k.shape
    return pl.pallas_call(
        _kv_expand_kernel,
        grid=(rows // tm,),
        in_specs=[pl.BlockSpec((tm, nkv), lambda i: (i, 0))] * 2,
        out_specs=list(_kv_specs(tm)),
        out_shape=list(_kv_shapes(rows)),
        compiler_params=_params("parallel"),
        name="kv_expand",
    )(k, v)


def _attn_kernel(*refs, tk, n_tiles, has_ctx):
    if has_ctx:
        q_ref, k_ref, v_ref, kc_ref, vc_ref, o_ref = refs
    else:
        q_ref, k_ref, v_ref, o_ref = refs
        kc_ref = vc_ref = None
    tq = q_ref.shape[0]
    lo = _lane_iota(tq) < HEAD_DIM
    tiles = [(k_ref, v_ref, t) for t in range(n_tiles)]
    if has_ctx:
        tiles += [(kc_ref, vc_ref, t) for t in range(vc_ref.shape[1] // tk)]
    for pair in range(KV_REP // 2):
        qp = q_ref[:, pair * LANES:(pair + 1) * LANES]
        outs = []
        for half in range(2):
            m = jnp.full((tq, 1), -jnp.inf, F32)
            acc = jnp.zeros((tq, LANES), F32)
            for kr, vr, t in tiles:
                s = jnp.dot(qp, kr[half, :, t * tk:(t + 1) * tk], preferred_element_type=F32)
                m_new = jnp.maximum(m, jnp.max(s, axis=-1, keepdims=True))
                p = jnp.exp2(s - m_new).astype(BF16)
                acc = jnp.exp2(m - m_new) * acc + jnp.dot(p, vr[half, t * tk:(t + 1) * tk, :],
                                                          preferred_element_type=F32)
                m = m_new
            l = acc[:, HEAD_DIM:HEAD_DIM + 1] if half == 0 else acc[:, 0:1]
            outs.append(acc / l)
        o_ref[:, pair * LANES:(pair + 1) * LANES] = jnp.where(lo, outs[0], outs[1]).astype(BF16)


def _attention(q, k2, v2, n_batch, t_len, ctx=None, tq=512, tk=512):
    tk = min(tk, t_len)
    tq = min(tq, t_len)
    nq = t_len // tq
    gw = KV_REP * HEAD_DIM
    q_spec = pl.BlockSpec((tq, gw), lambda b, g, i: (b * nq + i, g))
    in_specs = [q_spec,
                pl.BlockSpec((None, 2, LANES, t_len), lambda b, g, i: (g, 0, 0, b)),
                pl.BlockSpec((None, 2, t_len, LANES), lambda b, g, i: (g, 0, b, 0))]
    args = [q, k2, v2]
    if ctx is not None:
        kc2, vc2, ctx_len = ctx
        in_specs += [pl.BlockSpec((None, 2, LANES, ctx_len), lambda b, g, i: (g, 0, 0, b)),
                     pl.BlockSpec((None, 2, ctx_len, LANES), lambda b, g, i: (g, 0, b, 0))]
        args += [kc2, vc2]
    return pl.pallas_call(
        functools.partial(_attn_kernel, tk=tk, n_tiles=t_len // tk, has_ctx=ctx is not None),
        grid=(n_batch, N_KV_HEADS, nq),
        in_specs=in_specs,
        out_specs=q_spec,
        out_shape=jax.ShapeDtypeStruct(q.shape, BF16),
        compiler_params=_params("parallel", "parallel", "arbitrary"),
        name="attention",
    )(*args)


def _conv_kernel(u_ref, w_ref, b_ref, o_ref):
    u = u_ref[...]
    n = u.shape[0]
    row = lax.broadcasted_iota(jnp.int32, u.shape, 0)
    acc = u * w_ref[D_CONV // 2:D_CONV // 2 + 1, :] + b_ref[...]
    for k in range(D_CONV):
        d = k - D_CONV // 2
        if d == 0:
            continue
        shifted = pltpu.roll(u, (-d) % n, 0)
        valid = (row + d >= 0) & (row + d < n)
        acc = acc + jnp.where(valid, shifted, 0.0) * w_ref[k:k + 1, :]
    o_ref[...] = _silu(acc)


def _conv_silu(zx, col0, row0, n_seq, seq_len, w, b, tc):
    c = w.shape[1]
    coff, roff = col0 // tc, row0 // seq_len
    return pl.pallas_call(
        _conv_kernel,
        grid=(n_seq, c // tc),
        in_specs=[pl.BlockSpec((seq_len, tc), lambda s, j: (s + roff, j + coff)),
                  pl.BlockSpec((D_CONV, tc), lambda s, j: (0, j)),
                  pl.BlockSpec((1, tc), lambda s, j: (0, j))],
        out_specs=pl.BlockSpec((seq_len, tc), lambda s, j: (s, j)),
        out_shape=jax.ShapeDtypeStruct((n_seq * seq_len, c), F32),
        compiler_params=_params("parallel", "parallel"),
        name="conv_silu",
    )(zx, w, b)


def _cumsum_rows(v, reverse):
    n = v.shape[0]
    row = lax.broadcasted_iota(jnp.int32, v.shape, 0)
    k = 1
    while k < n:
        if reverse:
            v = v + jnp.where(row < n - k, pltpu.roll(v, n - k, 0), 0.0)
        else:
            v = v + jnp.where(row >= k, pltpu.roll(v, k, 0), 0.0)
        k *= 2
    return v


def _pair_cols(m, i):
    lo = _lane_iota(m.shape[0]) < SSD_HEAD_DIM
    return jnp.where(lo, m[:, i:i + 1], m[:, i + 1:i + 2])


def _ssd_kernel(*refs, rev, final_pass):
    it = iter(refs)
    x_ref, b_ref, c_ref, dt_ref, dtb_ref, alog_ref, init_ref = [next(it) for _ in range(7)]
    if final_pass:
        y0_ref, z_ref, dskip_ref = next(it), next(it), next(it)
    y_ref, fin_ref, st_ref = next(it), next(it), next(it)

    ci = pl.program_id(1)

    @pl.when(ci == 0)
    def _():
        st_ref[...] = init_ref[...]

    dirn = 1 if rev else 0
    dtv = jax.nn.softplus(dt_ref[...] + dtb_ref[...])
    da = dtv * (-jnp.exp(alog_ref[...]))
    acum = _cumsum_rows(da, rev)
    acum_t = acum.T
    last = 0 if rev else CHUNK - 1
    total = acum[last:last + 1, :]
    e_out = jnp.exp(acum)
    e_in = jnp.exp(total - acum)
    e_tot = jnp.exp(total)

    row = lax.broadcasted_iota(jnp.int32, (CHUNK, CHUNK), 0)
    col = lax.broadcasted_iota(jnp.int32, (CHUNK, CHUNK), 1)
    causal = (col >= row) if rev else (col <= row)
    lo = _lane_iota(CHUNK) < SSD_HEAD_DIM

    for g in range(SSD_GROUPS):
        bg = b_ref[:, g * D_STATE:(g + 1) * D_STATE].astype(BF16)
        cg = c_ref[:, g * D_STATE:(g + 1) * D_STATE].astype(BF16)
        cb = lax.dot_general(cg, bg, (((1,), (1,)), ((), ())), preferred_element_type=F32)
        for pp in range(SSD_REP // 2):
            h0 = g * SSD_REP + 2 * pp
            i0 = dirn * SSD_HEADS + h0
            cols = slice(h0 * SSD_HEAD_DIM, (h0 + 2) * SSD_HEAD_DIM)
            xp = x_ref[:, cols]
            xs = xp * _pair_cols(dtv, i0)
            xs_lo = jnp.where(lo, xs, 0.0).astype(BF16)
            xs_hi = jnp.where(lo, 0.0, xs).astype(BF16)
            y = None
            for hh, xh in ((0, xs_lo), (1, xs_hi)):
                a_col = acum[:, i0 + hh:i0 + hh + 1]
                a_row = acum_t[i0 + hh:i0 + hh + 1, :]
                lmat = jnp.exp(jnp.where(causal, a_col - a_row, -jnp.inf))
                yd = jnp.dot((cb * lmat).astype(BF16), xh, preferred_element_type=F32)
                y = yd if y is None else y + yd
            st_pair = st_ref[cols, :]
            y_off = lax.dot_general(cg, st_pair.astype(BF16), (((1,), (1,)), ((), ())),
                                    preferred_element_type=F32)
            y = y + y_off * _pair_cols(e_out, i0)
            xdec = (xs * _pair_cols(e_in, i0)).astype(BF16)
            new = lax.dot_general(xdec, bg, (((0,), (0,)), ((), ())), preferred_element_type=F32)
            half = lax.broadcasted_iota(jnp.int32, (2 * SSD_HEAD_DIM, D_STATE), 0) < SSD_HEAD_DIM
            dec = jnp.where(half, e_tot[:, i0:i0 + 1], e_tot[:, i0 + 1:i0 + 2])
            st_ref[cols, :] = st_pair * dec + new
            if final_pass:
                y = (y0_ref[:, cols] + y + xp * dskip_ref[:, cols]) * _silu(z_ref[:, cols])
            y_ref[:, cols] = y

    @pl.when(ci == pl.num_programs(1) - 1)
    def _():
        fin_ref[...] = st_ref[...]


def _ssd_scan(xbc, dt, dtb, alog, init, n_seq, seq_len, dt_row0, rev, final=None):
    nc = seq_len // CHUNK
    dirn = 1 if rev else 0
    dt_off = dt_row0 // CHUNK

    def tok(s, c):
        return s * nc + (nc - 1 - c if rev else c)

    nb = D_INNER // (SSD_GROUPS * D_STATE)
    in_specs = [pl.BlockSpec((CHUNK, D_INNER), lambda s, c: (tok(s, c), 0)),
                pl.BlockSpec((CHUNK, SSD_GROUPS * D_STATE), lambda s, c: (tok(s, c), nb)),
                pl.BlockSpec((CHUNK, SSD_GROUPS * D_STATE), lambda s, c: (tok(s, c), nb + 1)),
                pl.BlockSpec((CHUNK, LANES), lambda s, c: (tok(s, c) + dt_off, 0)),
                pl.BlockSpec((1, LANES), lambda s, c: (0, 0)),
                pl.BlockSpec((1, LANES), lambda s, c: (0, 0)),
                pl.BlockSpec((None, None, D_INNER, D_STATE), lambda s, c: (s, dirn, 0, 0))]
    args = [xbc, xbc, xbc, dt, dtb, alog, init]
    if final is not None:
        y0, zx, zx_row0, dskip = final
        z_off = zx_row0 // CHUNK
        in_specs += [pl.BlockSpec((CHUNK, D_INNER), lambda s, c: (tok(s, c), 0)),
                     pl.BlockSpec((CHUNK, D_INNER), lambda s, c: (tok(s, c) + z_off, 0)),
                     pl.BlockSpec((1, D_INNER), lambda s, c: (0, 0))]
        args += [y0, zx, dskip]
    return pl.pallas_call(
        functools.partial(_ssd_kernel, rev=rev, final_pass=final is not None),
        grid=(n_seq, nc),
        in_specs=in_specs,
        out_specs=[pl.BlockSpec((CHUNK, D_INNER), lambda s, c: (tok(s, c), 0)),
                   pl.BlockSpec((None, D_INNER, D_STATE), lambda s, c: (s, 0, 0))],
        out_shape=[jax.ShapeDtypeStruct((n_seq * seq_len, D_INNER), F32),
                   jax.ShapeDtypeStruct((n_seq, D_INNER, D_STATE), F32)],
        scratch_shapes=[pltpu.VMEM((D_INNER, D_STATE), F32)],
        compiler_params=_params("parallel", "arbitrary"),
        name="ssd_scan",
    )(*args)


def _route(biased_t, scores_t, gt_ref):
    ng, ne = N_EXPERT_GROUPS, EXPERTS_PER_GROUP
    b = [[biased_t[j * ne + k:j * ne + k + 1, :] for k in range(ne)] for j in range(ng)]
    s = [[scores_t[j * ne + k:j * ne + k + 1, :] for k in range(ne)] for j in range(ng)]
    gs = []
    for j in range(ng):
        best = None
        for k1 in range(ne):
            for k2 in range(k1 + 1, ne):
                ps = b[j][k1] + b[j][k2]
                best = ps if best is None else jnp.maximum(best, ps)
        gs.append(best)
    sel = jnp.zeros_like(gs[0], jnp.int32)
    best = gs[0]
    for j in range(1, ng):
        gt = gs[j] > best
        best = jnp.where(gt, gs[j], best)
        sel = jnp.where(gt, j, sel)
    in_grp = [sel == j for j in range(ng)]

    def pick(vals):
        out = []
        for k in range(ne):
            v = vals[ng - 1][k]
            for j in range(ng - 2, -1, -1):
                v = jnp.where(in_grp[j], vals[j][k], v)
            out.append(v)
        return out

    yb, ys = pick(b), pick(s)

    def argmax_first(vals):
        bv, bi = vals[0], jnp.zeros_like(sel)
        for k in range(1, ne):
            gt = vals[k] > bv
            bv = jnp.where(gt, vals[k], bv)
            bi = jnp.where(gt, k, bi)
        return bi

    i1 = argmax_first(yb)
    i2 = argmax_first([jnp.where(i1 == k, -jnp.inf, yb[k]) for k in range(ne)])

    def take(vals, idx):
        v = vals[ne - 1]
        for k in range(ne - 2, -1, -1):
            v = jnp.where(idx == k, vals[k], v)
        return v

    w1, w2 = take(ys, i1), take(ys, i2)
    tot = w1 + w2
    g1, g2 = w1 / tot, w2 / tot
    for j in range(ng):
        for k in range(ne):
            gate = jnp.where(in_grp[j] & (i1 == k), g1, 0.0) + jnp.where(in_grp[j] & (i2 == k), g2, 0.0)
            gt_ref[j * ne + k:j * ne + k + 1, :] = gate


def _moe_kernel(x_ref, g_ref, sh_ref, sc_ref, gate_ref, rw_ref, rb_ref, wg_ref, wu_ref, wd_ref,
                o_ref, h_ref, gates_ref, gt_ref, acc_ref):
    e = pl.program_id(1)

    @pl.when(e == 0)
    def _():
        h = _adaln(x_ref[...], g_ref[...], sh_ref[...], sc_ref[...]).astype(BF16)
        h_ref[...] = h
        logits_t = lax.dot_general(rw_ref[...], h, (((1,), (1,)), ((), ())),
                                   preferred_element_type=F32)
        scores_t = jax.nn.sigmoid(logits_t)
        gt_ref[...] = jnp.zeros_like(gt_ref)
        _route(scores_t + rb_ref[...], scores_t, gt_ref)
        gates_ref[...] = gt_ref[...].T
        acc_ref[...] = jnp.zeros_like(acc_ref)

    h = h_ref[...]
    hg = jnp.dot(h, wg_ref[...], preferred_element_type=F32)
    hu = jnp.dot(h, wu_ref[...], preferred_element_type=F32)
    lane = _lane_iota(h.shape[0])
    gate = jnp.sum(jnp.where(lane == e, gates_ref[...], 0.0), axis=-1, keepdims=True)
    act = (_silu(hg) * hu * gate).astype(BF16)
    acc_ref[...] += jnp.dot(act, wd_ref[...], preferred_element_type=F32)

    @pl.when(e == pl.num_programs(1) - 1)
    def _():
        o_ref[...] = x_ref[...] + gate_ref[...] * acc_ref[...]


def _moe(x, g, mod, layer, rw_t, rb_col, wg, wu, wd, tm=1024):
    t, d = x.shape
    ne, _, f = wg.shape
    return pl.pallas_call(
        _moe_kernel,
        grid=(t // tm, ne),
        in_specs=[pl.BlockSpec((tm, d), lambda i, e: (i, 0)),
                  pl.BlockSpec((1, d), lambda i, e: (0, 0)),
                  _mod_spec(layer, 3, tm, 4096),
                  _mod_spec(layer, 4, tm, 4096),
                  _mod_spec(layer, 5, tm, 4096),
                  pl.BlockSpec((LANES, d), lambda i, e: (0, 0)),
                  pl.BlockSpec((LANES, 1), lambda i, e: (0, 0)),
                  pl.BlockSpec((None, d, f), lambda i, e: (e, 0, 0)),
                  pl.BlockSpec((None, d, f), lambda i, e: (e, 0, 0)),
                  pl.BlockSpec((None, f, d), lambda i, e: (e, 0, 0))],
        out_specs=pl.BlockSpec((tm, d), lambda i, e: (i, 0)),
        out_shape=jax.ShapeDtypeStruct((t, d), F32),
        scratch_shapes=[pltpu.VMEM((tm, d), BF16), pltpu.VMEM((tm, LANES), F32),
                        pltpu.VMEM((LANES, tm), F32), pltpu.VMEM((tm, d), F32)],
        compiler_params=_params("parallel", "arbitrary"),
        name="moe",
    )(x, g, mod, mod, mod, rw_t, rb_col, wg, wu, wd)


def _rope_tables(t_len):
    rows = t_len // GRID_W
    row_ids = jnp.repeat(jnp.arange(rows), GRID_W).astype(F32)
    col_ids = jnp.tile(jnp.arange(GRID_W), rows).astype(F32)
    n_freq = HEAD_DIM // 4
    inv = 1.0 / (ROPE_THETA ** (jnp.arange(n_freq, dtype=F32) / n_freq))
    ang = jnp.concatenate([row_ids[:, None] * inv, col_ids[:, None] * inv], axis=-1)
    cos, sin = jnp.cos(ang), jnp.sin(ang)
    cs = jnp.tile(jnp.concatenate([cos, cos], axis=-1), (1, LANES // HEAD_DIM))
    sn = jnp.tile(jnp.concatenate([-sin, sin], axis=-1), (1, LANES // HEAD_DIM))
    return cs, sn


def _pad_lanes(v):
    return jnp.pad(v.reshape(1, -1), ((0, 0), (0, LANES - v.size)))


def kernel(x_prompt, x_sample, c, c_ctx, cache_k, cache_v, state_ssm, mod_w, mod_b, norm1_g, norm2_g,
           attn_w_qkv, attn_q_norm_g, attn_k_norm_g, attn_w_o, ssd_in_proj, ssd_conv_w, ssd_conv_b,
           ssd_dt_bias, ssd_a_log, ssd_d, ssd_norm_g, ssd_out_proj, router_w, router_b, expert_w_gate,
           expert_w_up, expert_w_down):
    n_p, l_p, d = x_prompt.shape
    n_s, l_s, _ = x_sample.shape
    t_p, t_s = n_p * l_p, n_s * l_s
    depth = mod_w.shape[0]
    past = cache_k.shape[2]
    nkv = N_KV_HEADS * HEAD_DIM
    assert t_p == 2 * l_s and n_s == 2 and d == D_MODEL

    x = jnp.concatenate([x_prompt.reshape(t_p, d), x_sample.reshape(t_s, d)], axis=0)
    cond8 = jnp.concatenate([c_ctx[None], c, jnp.zeros((8 - 1 - n_s, d), F32)], axis=0)
    mod = _modulation(cond8, mod_w, mod_b)
    mod = mod[:, :1 + n_s].reshape(depth, 1 + n_s, 6, 1, d).transpose(0, 2, 1, 3, 4)

    rope = _rope_tables(l_s)
    rw_t = jnp.pad(router_w.T, ((0, LANES - N_EXPERTS), (0, 0))).astype(BF16)
    rb_col = jnp.pad(router_b, (0, LANES - N_EXPERTS)).reshape(LANES, 1)
    wg_all, wu_all, wd_all = (w.astype(BF16) for w in (expert_w_gate, expert_w_up, expert_w_down))

    new_k, new_v, new_ssm = [], [], []
    for layer in range(depth):
        i = layer // 2
        g1 = norm1_g[layer][None]
        if layer % 2 == 0:
            qkv = _ada_mm(x, g1, mod, layer, attn_w_qkv[i].astype(BF16), tn=attn_w_qkv.shape[2])
            qg2 = jnp.tile(attn_q_norm_g[i], LANES // HEAD_DIM)[None]
            kg2 = jnp.tile(attn_k_norm_g[i], LANES // HEAD_DIM)[None]
            q_p, k2_p, v2_p, kc, vc = _qk_prep(qkv, 0, t_p, qg2, kg2, emit_cache=True)
            q_s, k2_s, v2_s = _qk_prep(qkv, t_p, t_s, qg2, kg2, tables=rope)
            kc2, vc2 = _kv_expand(cache_k[:, i].reshape(n_s * past, nkv), cache_v[:, i].reshape(n_s * past, nkv))
            o_p = _attention(q_p, k2_p, v2_p, n_p, l_p)
            o_s = _attention(q_s, k2_s, v2_s, n_s, l_s, ctx=(kc2, vc2, past))
            o = jnp.concatenate([o_p, o_s], axis=0)
            x = _mm_res(o, attn_w_o[i].astype(BF16), x, mod, layer)
            new_k.append(kc.reshape(n_p, l_p, N_KV_HEADS, HEAD_DIM))
            new_v.append(vc.reshape(n_p, l_p, N_KV_HEADS, HEAD_DIM))
        else:
            w_in = ssd_in_proj[i]
            nzx = D_INNER + CONV_DIM
            zx = _ada_mm(x, g1, mod, layer, w_in[:, :nzx].astype(BF16), tn=1024)
            w_dt = jnp.pad(w_in[:, nzx:], ((0, 0), (0, LANES - 2 * SSD_HEADS))).astype(BF16)
            dt = _ada_mm(x, g1, mod, layer, w_dt, tn=LANES)
            dtb = _pad_lanes(ssd_dt_bias[i])
            alog = _pad_lanes(ssd_a_log[i])
            dskip = jnp.repeat(ssd_d[i], SSD_HEAD_DIM)[None]
            cw, cb = ssd_conv_w[i], ssd_conv_b[i][None]
            ys = []
            for row0, n_seq, seq_len, init, tc in (
                    (0, n_p, l_p, jnp.zeros((n_p, 2, D_INNER, D_STATE), F32), 1024),
                    (t_p, n_s, l_s, state_ssm[:, i].reshape(n_s, 2, D_INNER, D_STATE), 256)):
                xbc = _conv_silu(zx, D_INNER, row0, n_seq, seq_len, cw, cb, tc)
                y_f, s_f = _ssd_scan(xbc, dt, dtb, alog, init, n_seq, seq_len, row0, rev=False)
                y, s_b = _ssd_scan(xbc, dt, dtb, alog, init, n_seq, seq_len, row0, rev=True,
                                   final=(y_f, zx, row0, dskip))
                ys.append(y)
                if row0 == 0:
                    new_ssm.append(jnp.stack([s_f, s_b], axis=1).reshape(
                        n_p, 2, SSD_HEADS, SSD_HEAD_DIM, D_STATE))
            yg = jnp.concatenate(ys, axis=0)
            x = _mm_res(yg, ssd_out_proj[i].astype(BF16), x, mod, layer, norm_g=ssd_norm_g[i][None])
        x = _moe(x, norm2_g[layer][None], mod, layer, rw_t, rb_col,
                 wg_all[layer], wu_all[layer], wd_all[layer])

    y_prompt = x[:t_p].reshape(n_p, l_p, d)
    y_sample = x[t_p:].reshape(n_s, l_s, d)
    return (y_prompt, y_sample, jnp.stack(new_k, axis=1), jnp.stack(new_v, axis=1),
            jnp.stack(new_ssm, axis=1))
```

```python
import functools
import math
from typing import NamedTuple

import jax
import jax.numpy as jnp
from jax import lax
from jax.experimental import pallas as pl
from jax.experimental.pallas import tpu as pltpu

F32 = jnp.float32
BF16 = jnp.bfloat16

D_MODEL = 1024
GRID_W = 64
N_HEADS = 16
N_KV_HEADS = 4
HEAD_DIM = 64
KV_REP = N_HEADS // N_KV_HEADS
ROPE_THETA = 10000.0
D_INNER = 2048
SSD_HEAD_DIM = 64
SSD_HEADS = 32
SSD_GROUPS = 4
SSD_REP = SSD_HEADS // SSD_GROUPS
D_STATE = 128
D_CONV = 5
CHUNK = 128
CONV_DIM = D_INNER + 2 * SSD_GROUPS * D_STATE
N_EXPERTS = 16
N_EXPERT_GROUPS = 4
EXPERTS_PER_GROUP = 4
D_FF_EXPERT = 256
EPS = 1e-6

LANES = 128
VMEM_LIMIT = 56 * 1024 * 1024


def _params(*sem):
    return pltpu.CompilerParams(dimension_semantics=sem, vmem_limit_bytes=VMEM_LIMIT)


def _silu(x):
    return x * jax.nn.sigmoid(x)


class _Stream(NamedTuple):
    n_seq: int
    seq_len: int
    mod_base: int
    mod_step: int

    @property
    def rows(self):
        return self.n_seq * self.seq_len


def _mod_kernel(c_ref, w_ref, b_ref, o_ref):
    s = _silu(c_ref[...])
    o_ref[...] = jnp.dot(s.astype(BF16), w_ref[...].astype(BF16),
                         preferred_element_type=F32) + b_ref[...]


def _modulation(cond8, mod_w, mod_b):
    depth, d, n = mod_w.shape
    tn = 1536
    return pl.pallas_call(
        _mod_kernel,
        grid=(depth, n // tn),
        in_specs=[pl.BlockSpec((8, d), lambda l, j: (0, 0)),
                  pl.BlockSpec((None, d, tn), lambda l, j: (l, 0, j)),
                  pl.BlockSpec((None, 1, tn), lambda l, j: (l, 0, j))],
        out_specs=pl.BlockSpec((None, 8, tn), lambda l, j: (l, 0, j)),
        out_shape=jax.ShapeDtypeStruct((depth, 8, n), F32),
        compiler_params=_params("parallel", "parallel"),
        name="modulation",
    )(cond8, mod_w, mod_b.reshape(depth, 1, n))


def _mod_spec(layer, chunk, tm, st):
    def imap(i, *_):
        return (layer, chunk, st.mod_base + st.mod_step * (i * tm // st.seq_len), 0, 0)
    return pl.BlockSpec((None, None, None, 1, D_MODEL), imap)


def _adaln(x, g, sh, sc):
    ms = jnp.mean(x * x, axis=-1, keepdims=True)
    y = x * lax.rsqrt(ms + EPS) * g
    return y * (1 + sc) + sh


def _ada_mm_kernel(*refs, has_side):
    if has_side:
        x_ref, g_ref, sh_ref, sc_ref, w_ref, ws_ref, o_ref, os_ref, h_ref = refs
    else:
        x_ref, g_ref, sh_ref, sc_ref, w_ref, o_ref, h_ref = refs

    @pl.when(pl.program_id(1) == 0)
    def _():
        h = _adaln(x_ref[...], g_ref[...], sh_ref[...], sc_ref[...]).astype(BF16)
        h_ref[...] = h
        if has_side:
            os_ref[...] = jnp.dot(h, ws_ref[...], preferred_element_type=F32)

    o_ref[...] = jnp.dot(h_ref[...], w_ref[...], preferred_element_type=F32).astype(o_ref.dtype)


def _ada_mm(x, g, mod, layer, st, w, tn, w_side=None, tm=512):
    t, d = x.shape
    n = w.shape[1]
    in_specs = [pl.BlockSpec((tm, d), lambda i, j: (i, 0)),
                pl.BlockSpec((1, d), lambda i, j: (0, 0)),
                _mod_spec(layer, 0, tm, st),
                _mod_spec(layer, 1, tm, st),
                pl.BlockSpec((d, tn), lambda i, j: (0, j))]
    args = [x, g, mod, mod, w]
    out_specs = [pl.BlockSpec((tm, tn), lambda i, j: (i, j))]
    out_shape = [jax.ShapeDtypeStruct((t, n), F32)]
    if w_side is not None:
        ns = w_side.shape[1]
        in_specs.append(pl.BlockSpec((d, ns), lambda i, j: (0, 0)))
        args.append(w_side)
        out_specs.append(pl.BlockSpec((tm, ns), lambda i, j: (i, 0)))
        out_shape.append(jax.ShapeDtypeStruct((t, ns), F32))
    out = pl.pallas_call(
        functools.partial(_ada_mm_kernel, has_side=w_side is not None),
        grid=(t // tm, n // tn),
        in_specs=in_specs,
        out_specs=out_specs,
        out_shape=out_shape,
        scratch_shapes=[pltpu.VMEM((tm, d), BF16)],
        compiler_params=_params("parallel", "arbitrary"),
        name="ada_mm",
    )(*args)
    return out if w_side is not None else out[0]


def _mm_res_kernel(a_ref, w_ref, r_ref, gate_ref, o_ref):
    y = jnp.dot(a_ref[...], w_ref[...], preferred_element_type=F32)
    o_ref[...] = r_ref[...] + gate_ref[...] * y


def _norm_mm_res_kernel(a_ref, ng_ref, w_ref, r_ref, gate_ref, o_ref):
    a = a_ref[...]
    ms = jnp.mean(a * a, axis=-1, keepdims=True)
    an = (a * lax.rsqrt(ms + EPS) * ng_ref[...]).astype(BF16)
    y = jnp.dot(an, w_ref[...], preferred_element_type=F32)
    o_ref[...] = r_ref[...] + gate_ref[...] * y


def _mm_res(a, w, res, mod, layer, st, norm_g=None, tm=512):
    t, k = a.shape
    n = w.shape[1]
    a_spec = pl.BlockSpec((tm, k), lambda i: (i, 0))
    w_spec = pl.BlockSpec((k, n), lambda i: (0, 0))
    r_spec = pl.BlockSpec((tm, n), lambda i: (i, 0))
    gate_spec = _mod_spec(layer, 2, tm, st)
    if norm_g is None:
        kern, specs, args = _mm_res_kernel, [a_spec, w_spec, r_spec, gate_spec], (a, w, res, mod)
    else:
        kern = _norm_mm_res_kernel
        specs = [a_spec, pl.BlockSpec((1, k), lambda i: (0, 0)), w_spec, r_spec, gate_spec]
        args = (a, norm_g, w, res, mod)
    return pl.pallas_call(
        kern,
        grid=(t // tm,),
        in_specs=specs,
        out_specs=pl.BlockSpec((tm, n), lambda i: (i, 0)),
        out_shape=jax.ShapeDtypeStruct((t, n), F32),
        compiler_params=_params("parallel"),
        name="mm_res",
    )(*args)


def _lane_iota(rows):
    return lax.broadcasted_iota(jnp.int32, (rows, LANES), 1)


def _head_pair_norm(c, g2):
    lo = _lane_iota(c.shape[0]) < HEAD_DIM
    cc = c * c
    s_lo = jnp.sum(jnp.where(lo, cc, 0.0), axis=-1, keepdims=True)
    s_hi = jnp.sum(jnp.where(lo, 0.0, cc), axis=-1, keepdims=True)
    r = jnp.where(lo, lax.rsqrt(s_lo / HEAD_DIM + EPS), lax.rsqrt(s_hi / HEAD_DIM + EPS))
    return c * r * g2


def _rope_pair(y, cs, sn):
    first = (_lane_iota(y.shape[0]) % HEAD_DIM) < HEAD_DIM // 2
    partner = jnp.where(first, pltpu.roll(y, LANES - HEAD_DIM // 2, 1), pltpu.roll(y, HEAD_DIM // 2, 1))
    return y * cs + partner * sn


def _store_v_split(ref, j, c):
    lane = _lane_iota(c.shape[0])
    lo = lane < HEAD_DIM
    a_lo = jnp.where(lo, c, jnp.where(lane == HEAD_DIM, 1.0, 0.0))
    b_hi = jnp.where(lo, jnp.where(lane == 0, 1.0, 0.0), c)
    ref[2 * j, 0] = a_lo.astype(BF16)
    ref[2 * j, 1] = pltpu.roll(a_lo, HEAD_DIM, 1).astype(BF16)
    ref[2 * j + 1, 0] = pltpu.roll(b_hi, HEAD_DIM, 1).astype(BF16)
    ref[2 * j + 1, 1] = b_hi.astype(BF16)


def _store_kt_split(ref, j, c):
    t = c.T
    a, b = t[:HEAD_DIM], t[HEAD_DIM:]
    z = jnp.zeros_like(a)
    ref[2 * j, 0] = jnp.concatenate([a, z], axis=0).astype(BF16)
    ref[2 * j, 1] = jnp.concatenate([z, a], axis=0).astype(BF16)
    ref[2 * j + 1, 0] = jnp.concatenate([b, z], axis=0).astype(BF16)
    ref[2 * j + 1, 1] = jnp.concatenate([z, b], axis=0).astype(BF16)


def _qk_prep_kernel(*refs, rope, emit_cache):
    it = iter(refs)
    qkv_ref, qg_ref, kg_ref = next(it), next(it), next(it)
    cs_ref, sn_ref = (next(it), next(it)) if rope else (None, None)
    q_ref, k2_ref, v2_ref = next(it), next(it), next(it)
    kc_ref, vc_ref = (next(it), next(it)) if emit_cache else (None, None)

    nq = N_HEADS * HEAD_DIM
    nkv = N_KV_HEADS * HEAD_DIM
    scale = HEAD_DIM ** -0.5 * math.log2(math.e)
    for j in range(nq // LANES):
        y = _head_pair_norm(qkv_ref[:, j * LANES:(j + 1) * LANES], qg_ref[...])
        if rope:
            y = _rope_pair(y, cs_ref[...], sn_ref[...])
        q_ref[:, j * LANES:(j + 1) * LANES] = (y * scale).astype(BF16)
    for j in range(nkv // LANES):
        y = _head_pair_norm(qkv_ref[:, nq + j * LANES:nq + (j + 1) * LANES], kg_ref[...])
        if rope:
            y = _rope_pair(y, cs_ref[...], sn_ref[...])
        if emit_cache:
            kc_ref[:, j * LANES:(j + 1) * LANES] = y
        _store_kt_split(k2_ref, j, y)
        v = qkv_ref[:, nq + nkv + j * LANES:nq + nkv + (j + 1) * LANES]
        if emit_cache:
            vc_ref[:, j * LANES:(j + 1) * LANES] = v
        _store_v_split(v2_ref, j, v)


def _kv_shapes(rows):
    return (jax.ShapeDtypeStruct((N_KV_HEADS, 2, LANES, rows), BF16),
            jax.ShapeDtypeStruct((N_KV_HEADS, 2, rows, LANES), BF16))


def _kv_specs(tm):
    return (pl.BlockSpec((N_KV_HEADS, 2, LANES, tm), lambda i: (0, 0, 0, i)),
            pl.BlockSpec((N_KV_HEADS, 2, tm, LANES), lambda i: (0, 0, i, 0)))


def _qk_prep(qkv, qg2, kg2, tables=None, emit_cache=False, tm=512):
    rows, n = qkv.shape
    nq = N_HEADS * HEAD_DIM
    nkv = N_KV_HEADS * HEAD_DIM
    rope = tables is not None
    in_specs = [pl.BlockSpec((tm, n), lambda i: (i, 0)),
                pl.BlockSpec((1, LANES), lambda i: (0, 0)),
                pl.BlockSpec((1, LANES), lambda i: (0, 0))]
    args = [qkv, qg2, kg2]
    if rope:
        nt = tables[0].shape[0] // tm
        in_specs += [pl.BlockSpec((tm, LANES), lambda i: (i % nt, 0))] * 2
        args += list(tables)
    out_shape = [jax.ShapeDtypeStruct((rows, nq), BF16), *_kv_shapes(rows)]
    out_specs = [pl.BlockSpec((tm, nq), lambda i: (i, 0)), *_kv_specs(tm)]
    if emit_cache:
        out_shape += [jax.ShapeDtypeStruct((rows, nkv), F32)] * 2
        out_specs += [pl.BlockSpec((tm, nkv), lambda i: (i, 0))] * 2
    return pl.pallas_call(
        functools.partial(_qk_prep_kernel, rope=rope, emit_cache=emit_cache),
        grid=(rows // tm,),
        in_specs=in_specs,
        out_specs=out_specs,
        out_shape=out_shape,
        compiler_params=_params("parallel"),
        name="qk_prep",
    )(*args)


def _kv_expand_kernel(k_ref, v_ref, k2_ref, v2_ref):
    for j in range(N_KV_HEADS * HEAD_DIM // LANES):
        _store_kt_split(k2_ref, j, k_ref[:, j * LANES:(j + 1) * LANES])
        _store_v_split(v2_ref, j, v_ref[:, j * LANES:(j + 1) * LANES])


def _kv_expand(k, v, tm=512):
    rows, nkv = k.shape
    return pl.pallas_call(
        _kv_expand_kernel,
        grid=(rows // tm,),
        in_specs=[pl.BlockSpec((tm, nkv), lambda i: (i, 0))] * 2,
        out_specs=list(_kv_specs(tm)),
        out_shape=list(_kv_shapes(rows)),
        compiler_params=_params("parallel"),
        name="kv_expand",
    )(k, v)


def _attn_kernel(*refs, tk, n_tiles, has_ctx):
    if has_ctx:
        q_ref, k_ref, v_ref, kc_ref, vc_ref, o_ref = refs
    else:
        q_ref, k_ref, v_ref, o_ref = refs
        kc_ref = vc_ref = None
    tq = q_ref.shape[0]
    lo = _lane_iota(tq) < HEAD_DIM
    tiles = [(k_ref, v_ref, t) for t in range(n_tiles)]
    if has_ctx:
        tiles += [(kc_ref, vc_ref, t) for t in range(vc_ref.shape[1] // tk)]
    for pair in range(KV_REP // 2):
        qp = q_ref[:, pair * LANES:(pair + 1) * LANES]
        outs = []
        for half in range(2):
            m = jnp.full((tq, 1), -jnp.inf, F32)
            acc = jnp.zeros((tq, LANES), F32)
            for kr, vr, t in tiles:
                s = jnp.dot(qp, kr[half, :, t * tk:(t + 1) * tk], preferred_element_type=F32)
                m_new = jnp.maximum(m, jnp.max(s, axis=-1, keepdims=True))
                p = jnp.exp2(s - m_new).astype(BF16)
                acc = jnp.exp2(m - m_new) * acc + jnp.dot(p, vr[half, t * tk:(t + 1) * tk, :],
                                                          preferred_element_type=F32)
                m = m_new
            l = acc[:, HEAD_DIM:HEAD_DIM + 1] if half == 0 else acc[:, 0:1]
            outs.append(acc / l)
        o_ref[:, pair * LANES:(pair + 1) * LANES] = jnp.where(lo, outs[0], outs[1]).astype(BF16)


def _attention(q, k2, v2, n_batch, t_len, ctx=None, tq=512, tk=512):
    tk = min(tk, t_len)
    tq = min(tq, t_len)
    nq = t_len // tq
    gw = KV_REP * HEAD_DIM
    q_spec = pl.BlockSpec((tq, gw), lambda b, g, i: (b * nq + i, g))
    in_specs = [q_spec,
                pl.BlockSpec((None, 2, LANES, t_len), lambda b, g, i: (g, 0, 0, b)),
                pl.BlockSpec((None, 2, t_len, LANES), lambda b, g, i: (g, 0, b, 0))]
    args = [q, k2, v2]
    if ctx is not None:
        kc2, vc2, ctx_len = ctx
        in_specs += [pl.BlockSpec((None, 2, LANES, ctx_len), lambda b, g, i: (g, 0, 0, b)),
                     pl.BlockSpec((None, 2, ctx_len, LANES), lambda b, g, i: (g, 0, b, 0))]
        args += [kc2, vc2]
    return pl.pallas_call(
        functools.partial(_attn_kernel, tk=tk, n_tiles=t_len // tk, has_ctx=ctx is not None),
        grid=(n_batch, N_KV_HEADS, nq),
        in_specs=in_specs,
        out_specs=q_spec,
        out_shape=jax.ShapeDtypeStruct(q.shape, BF16),
        compiler_params=_params("parallel", "parallel", "arbitrary"),
        name="attention",
    )(*args)


def _conv_kernel(u_ref, w_ref, b_ref, o_ref):
    u = u_ref[...]
    n = u.shape[0]
    row = lax.broadcasted_iota(jnp.int32, u.shape, 0)
    acc = u * w_ref[D_CONV // 2:D_CONV // 2 + 1, :] + b_ref[...]
    for k in range(D_CONV):
        d = k - D_CONV // 2
        if d == 0:
            continue
        shifted = pltpu.roll(u, (-d) % n, 0)
        valid = (row + d >= 0) & (row + d < n)
        acc = acc + jnp.where(valid, shifted, 0.0) * w_ref[k:k + 1, :]
    o_ref[...] = _silu(acc)


def _conv_silu(zx, col0, st, w, b, tc):
    c = w.shape[1]
    coff = col0 // tc
    return pl.pallas_call(
        _conv_kernel,
        grid=(st.n_seq, c // tc),
        in_specs=[pl.BlockSpec((st.seq_len, tc), lambda s, j: (s, j + coff)),
                  pl.BlockSpec((D_CONV, tc), lambda s, j: (0, j)),
                  pl.BlockSpec((1, tc), lambda s, j: (0, j))],
        out_specs=pl.BlockSpec((st.seq_len, tc), lambda s, j: (s, j)),
        out_shape=jax.ShapeDtypeStruct((st.rows, c), F32),
        compiler_params=_params("parallel", "parallel"),
        name="conv_silu",
    )(zx, w, b)


def _cumsum_rows(v, reverse):
    n = v.shape[0]
    row = lax.broadcasted_iota(jnp.int32, v.shape, 0)
    k = 1
    while k < n:
        if reverse:
            v = v + jnp.where(row < n - k, pltpu.roll(v, n - k, 0), 0.0)
        else:
            v = v + jnp.where(row >= k, pltpu.roll(v, k, 0), 0.0)
        k *= 2
    return v


def _ssd_kernel(*refs, rev, final_pass, has_init, cps):
    it = iter(refs)
    x_ref, b_ref, c_ref, dt_ref, dtb_ref, alog_ref = [next(it) for _ in range(6)]
    init_ref = next(it) if has_init else None
    if final_pass:
        y0_ref, z_ref, dskip_ref = next(it), next(it), next(it)
    y_ref, fin_ref, st_ref = next(it), next(it), next(it)

    ci = pl.program_id(1)
    n_blk = D_INNER // LANES

    @pl.when(ci == 0)
    def _():
        if has_init:
            for k in range(n_blk):
                st_ref[:, k * LANES:(k + 1) * LANES] = init_ref[k * LANES:(k + 1) * LANES, :].T
        else:
            st_ref[...] = jnp.zeros_like(st_ref)

    dirn = 1 if rev else 0
    neg_a = -jnp.exp(alog_ref[...])
    row = lax.broadcasted_iota(jnp.int32, (CHUNK, CHUNK), 0)
    col = lax.broadcasted_iota(jnp.int32, (CHUNK, CHUNK), 1)
    causal = (col >= row) if rev else (col <= row)
    lo = _lane_iota(CHUNK) < SSD_HEAD_DIM
    last = 0 if rev else CHUNK - 1

    for cc in (reversed(range(cps)) if rev else range(cps)):
        rows = slice(cc * CHUNK, (cc + 1) * CHUNK)
        dtv = jax.nn.softplus(dt_ref[rows, :] + dtb_ref[...])
        acum = _cumsum_rows(dtv * neg_a, rev) * math.log2(math.e)
        acum_t = acum.T
        dtv_t = dtv.T
        w_t = dtv_t * jnp.exp2(acum_t[:, last:last + 1] - acum_t)
        e_tot = jnp.exp2(acum[last:last + 1, :])
        src_t = acum_t - jnp.log2(dtv_t)
        for g in range(SSD_GROUPS):
            bg = b_ref[rows, g * D_STATE:(g + 1) * D_STATE]
            cg = c_ref[rows, g * D_STATE:(g + 1) * D_STATE]
            cb = lax.dot_general(cg.astype(BF16), bg.astype(BF16), (((1,), (1,)), ((), ())),
                                 preferred_element_type=F32)
            bg_t = bg.T
            for pp in range(SSD_REP // 2):
                h0 = g * SSD_REP + 2 * pp
                i0 = dirn * SSD_HEADS + h0
                cols = slice(h0 * SSD_HEAD_DIM, (h0 + 2) * SSD_HEAD_DIM)
                xp = x_ref[rows, cols]
                st_pair = st_ref[:, cols]
                halves = ((jnp.where(lo, xp, 0.0).astype(BF16), jnp.where(lo, st_pair, 0.0).astype(BF16)),
                          (jnp.where(lo, 0.0, xp).astype(BF16), jnp.where(lo, 0.0, st_pair).astype(BF16)))
                y = new = None
                for hh, (xh, sth) in enumerate(halves):
                    i = i0 + hh
                    a_col = jnp.broadcast_to(acum[:, i:i + 1], (CHUNK, CHUNK))
                    lmat = jnp.exp2(jnp.where(causal, a_col - src_t[i:i + 1, :], -jnp.inf))
                    m = (cb * lmat).astype(BF16)
                    ce = (cg * jnp.exp2(a_col)).astype(BF16)
                    yh = jnp.dot(jnp.concatenate([m, ce], axis=1), jnp.concatenate([xh, sth], axis=0),
                                 preferred_element_type=F32)
                    nh = jnp.dot((bg_t * w_t[i:i + 1, :]).astype(BF16), xh, preferred_element_type=F32)
                    y = yh if y is None else y + yh
                    new = nh if new is None else new + nh
                dec = jnp.where(lo[:1], e_tot[:, i0:i0 + 1], e_tot[:, i0 + 1:i0 + 2])
                st_ref[:, cols] = st_pair * dec + new
                if final_pass:
                    y = (y0_ref[rows, cols] + y + xp * dskip_ref[:, cols]) * _silu(z_ref[rows, cols])
                y_ref[rows, cols] = y

    @pl.when(ci == pl.num_programs(1) - 1)
    def _():
        for k in range(n_blk):
            fin_ref[k * LANES:(k + 1) * LANES, :] = st_ref[:, k * LANES:(k + 1) * LANES].T


def _ssd_scan(xbc, dt, dtb, alog, st, rev, init=None, final=None, cps=4):
    cps = min(cps, st.seq_len // CHUNK)
    blk = cps * CHUNK
    nb = st.seq_len // blk
    dirn = 1 if rev else 0

    def tok(s, c):
        return (s * nb + (nb - 1 - c if rev else c), 0)

    bc_w = SSD_GROUPS * D_STATE
    b_col = D_INNER // bc_w
    tok_spec = pl.BlockSpec((blk, D_INNER), tok)
    in_specs = [tok_spec,
                pl.BlockSpec((blk, bc_w), lambda s, c: (tok(s, c)[0], b_col)),
                pl.BlockSpec((blk, bc_w), lambda s, c: (tok(s, c)[0], b_col + 1)),
                pl.BlockSpec((blk, LANES), tok),
                pl.BlockSpec((1, LANES), lambda s, c: (0, 0)),
                pl.BlockSpec((1, LANES), lambda s, c: (0, 0))]
    args = [xbc, xbc, xbc, dt, dtb, alog]
    if init is not None:
        in_specs.append(pl.BlockSpec((None, None, D_INNER, D_STATE), lambda s, c: (s, dirn, 0, 0)))
        args.append(init)
    if final is not None:
        y0, zx, dskip = final
        in_specs += [tok_spec, tok_spec, pl.BlockSpec((1, D_INNER), lambda s, c: (0, 0))]
        args += [y0, zx, dskip]
    return pl.pallas_call(
        functools.partial(_ssd_kernel, rev=rev, final_pass=final is not None,
                          has_init=init is not None, cps=cps),
        grid=(st.n_seq, nb),
        in_specs=in_specs,
        out_specs=[tok_spec, pl.BlockSpec((None, D_INNER, D_STATE), lambda s, c: (s, 0, 0))],
        out_shape=[jax.ShapeDtypeStruct((st.rows, D_INNER), F32),
                   jax.ShapeDtypeStruct((st.n_seq, D_INNER, D_STATE), F32)],
        scratch_shapes=[pltpu.VMEM((D_STATE, D_INNER), F32)],
        compiler_params=_params("parallel", "arbitrary"),
        name="ssd_scan",
    )(*args)


MOE_TILE = 256
PAIRS_PER_GROUP = EXPERTS_PER_GROUP * (EXPERTS_PER_GROUP - 1) // 2
N_BUCKETS = N_EXPERT_GROUPS * PAIRS_PER_GROUP
BUCKET_ROWS = 32
TAIL = LANES


def _route_picks(biased_t, scores_t):
    ng, ne = N_EXPERT_GROUPS, EXPERTS_PER_GROUP
    b = [[biased_t[j * ne + k:j * ne + k + 1, :] for k in range(ne)] for j in range(ng)]
    s = [[scores_t[j * ne + k:j * ne + k + 1, :] for k in range(ne)] for j in range(ng)]
    gs = []
    for j in range(ng):
        best = None
        for k1 in range(ne):
            for k2 in range(k1 + 1, ne):
                ps = b[j][k1] + b[j][k2]
                best = ps if best is None else jnp.maximum(best, ps)
        gs.append(best)
    sel = jnp.zeros_like(gs[0], jnp.int32)
    best = gs[0]
    for j in range(1, ng):
        gt = gs[j] > best
        best = jnp.where(gt, gs[j], best)
        sel = jnp.where(gt, j, sel)
    in_grp = [sel == j for j in range(ng)]

    def pick(vals):
        out = []
        for k in range(ne):
            v = vals[ng - 1][k]
            for j in range(ng - 2, -1, -1):
                v = jnp.where(in_grp[j], vals[j][k], v)
            out.append(v)
        return out

    yb, ys = pick(b), pick(s)

    def argmax_first(vals):
        bv, bi = vals[0], jnp.zeros_like(sel)
        for k in range(1, ne):
            gt = vals[k] > bv
            bv = jnp.where(gt, vals[k], bv)
            bi = jnp.where(gt, k, bi)
        return bi

    i1 = argmax_first(yb)
    i2 = argmax_first([jnp.where(i1 == k, -jnp.inf, yb[k]) for k in range(ne)])

    def take(vals, idx):
        v = vals[ne - 1]
        for k in range(ne - 2, -1, -1):
            v = jnp.where(idx == k, vals[k], v)
        return v

    w1, w2 = take(ys, i1), take(ys, i2)
    tot = w1 + w2
    return sel, i1, i2, w1 / tot, w2 / tot


def _two_stream_specs(streams, tm, width):
    n0 = streams[0].rows // tm
    return (pl.BlockSpec((tm, width), lambda i, *_: (jnp.minimum(i, n0 - 1), 0)),
            pl.BlockSpec((tm, width), lambda i, *_: (jnp.maximum(i - n0, 0), 0)))


def _moe_route_kernel(x0_ref, x1_ref, g_ref, sh0_ref, sc0_ref, sh1_ref, sc1_ref, rw_ref, rb_ref,
                      tail_ref, bucket_ref, tt_ref, *, n0, mod_of_tile):
    i = pl.program_id(0)
    first = i < n0
    x = jnp.where(first, x0_ref[...], x1_ref[...])
    sh = jnp.where(first, sh0_ref[...], sh1_ref[...])
    sc = jnp.where(first, sc0_ref[...], sc1_ref[...])
    h = _adaln(x, g_ref[...], sh, sc).astype(BF16)
    logits_t = lax.dot_general(rw_ref[...], h, (((1,), (1,)), ((), ())),
                               preferred_element_type=F32)
    scores_t = jax.nn.sigmoid(logits_t)
    sel, i1, i2, g1, g2 = _route_picks(scores_t + rb_ref[...], scores_t)
    first_lo = i1 < i2
    klo = jnp.where(first_lo, i1, i2)
    khi = jnp.where(first_lo, i2, i1)
    pair = jnp.where(klo == 0, 0, jnp.where(klo == 1, 3, 5)) + (khi - klo - 1)
    bucket_ref[...] = sel * PAIRS_PER_GROUP + pair
    tt_ref[...] = jnp.zeros_like(tt_ref)
    tt_ref[0:1, :] = jnp.where(first_lo, g1, g2)
    tt_ref[1:2, :] = jnp.where(first_lo, g2, g1)
    tt_ref[2:3, :] = jnp.full(g1.shape, 1.0, F32) * mod_of_tile(i).astype(F32)
    tail_ref[...] = tt_ref[...].T


def _moe_route(xs, streams, g, mod, layer, rw_t, rb_col, tm=1024):
    d = xs[0].shape[1]
    t = sum(st.rows for st in streams)
    n0 = streams[0].rows // tm
    s0, s1 = streams

    def mod_of_tile(i):
        row0 = s0.mod_base + s0.mod_step * (i * tm // s0.seq_len)
        row1 = s1.mod_base + s1.mod_step * ((i - n0) * tm // s1.seq_len)
        return jnp.where(i < n0, row0, row1)

    def mspec(chunk, st, shift):
        def imap(i):
            return (layer, chunk, st.mod_base + st.mod_step * (jnp.maximum(i - shift, 0) * tm // st.seq_len), 0, 0)
        return pl.BlockSpec((None, None, None, 1, d), imap)

    x_specs = _two_stream_specs(streams, tm, d)
    return pl.pallas_call(
        functools.partial(_moe_route_kernel, n0=n0, mod_of_tile=mod_of_tile),
        grid=(t // tm,),
        in_specs=[*x_specs,
                  pl.BlockSpec((1, d), lambda i: (0, 0)),
                  mspec(3, s0, 0), mspec(4, s0, 0), mspec(3, s1, n0), mspec(4, s1, n0),
                  pl.BlockSpec((LANES, d), lambda i: (0, 0)),
                  pl.BlockSpec((LANES, 1), lambda i: (0, 0))],
        out_specs=[pl.BlockSpec((tm, TAIL), lambda i: (i, 0)),
                   pl.BlockSpec((1, tm), lambda i: (0, i))],
        out_shape=[jax.ShapeDtypeStruct((t, TAIL), F32), jax.ShapeDtypeStruct((1, t), jnp.int32)],
        scratch_shapes=[pltpu.VMEM((LANES, tm), F32)],
        compiler_params=_params("parallel"),
        name="moe_route",
    )(*xs, g, mod, mod, mod, mod, rw_t, rb_col)


PLAN_E_LO, PLAN_E_HI, PLAN_N_TILES, PLAN_ROWS = 0, 1, 2, 8


def _moe_plan_kernel(bucket_ref, pos_ref, plan_ref, *, blk):
    n_tok = bucket_ref.shape[1]
    bid = lax.broadcasted_iota(jnp.int32, (BUCKET_ROWS, blk), 0)
    r = lax.broadcasted_iota(jnp.int32, (blk, blk), 0)
    c = lax.broadcasted_iota(jnp.int32, (blk, blk), 1)
    upper = (r <= c).astype(BF16)

    def onehot(k):
        return (bucket_ref[:, k * blk:(k + 1) * blk] == bid).astype(F32)

    counts = jnp.zeros((BUCKET_ROWS, 1), F32)
    for k in range(n_tok // blk):
        counts = counts + jnp.sum(onehot(k), axis=1, keepdims=True)
    padded = jnp.floor((counts + (MOE_TILE - 1)) * (1.0 / MOE_TILE)) * MOE_TILE
    padded_b = jnp.broadcast_to(padded, (BUCKET_ROWS, LANES))
    starts_b = _cumsum_rows(padded_b, False) - padded_b
    starts = starts_b[:, 0:1]

    carry = jnp.zeros((BUCKET_ROWS, 1), F32)
    for k in range(n_tok // blk):
        oh = onehot(k)
        incl = jnp.dot(oh.astype(BF16), upper, preferred_element_type=F32)
        slot = jnp.sum(oh * (starts + carry + incl - 1.0), axis=0, keepdims=True)
        pos_ref[:, k * blk:(k + 1) * blk] = slot.astype(jnp.int32)
        carry = carry + incl[:, blk - 1:blk]

    tile0 = (lax.broadcasted_iota(jnp.int32, (BUCKET_ROWS, LANES), 1) * MOE_TILE).astype(F32)
    inside = (tile0 >= starts_b) & (tile0 < starts_b + padded_b)
    brow = lax.broadcasted_iota(jnp.int32, (BUCKET_ROWS, LANES), 0).astype(F32)
    tb = jnp.sum(jnp.where(inside, brow, 0.0), axis=0, keepdims=True).astype(jnp.int32)
    n_tiles = jnp.sum(jnp.sum(inside.astype(F32), axis=0, keepdims=True),
                      axis=1, keepdims=True).astype(jnp.int32)
    one = jnp.ones_like(tb)
    grp = jnp.where(tb >= 3 * PAIRS_PER_GROUP, 3 * one,
                    jnp.where(tb >= 2 * PAIRS_PER_GROUP, 2 * one, jnp.where(tb >= PAIRS_PER_GROUP, one, 0 * one)))
    p = tb - grp * PAIRS_PER_GROUP
    klo = jnp.where(p >= 5, 2 * one, jnp.where(p >= 3, one, 0 * one))
    khi = p - jnp.where(klo == 0, 0, jnp.where(klo == 1, 3, 5)) + klo + 1
    plan_ref[...] = jnp.zeros_like(plan_ref)
    plan_ref[PLAN_E_LO:PLAN_E_LO + 1, :] = grp * EXPERTS_PER_GROUP + klo
    plan_ref[PLAN_E_HI:PLAN_E_HI + 1, :] = grp * EXPERTS_PER_GROUP + khi
    plan_ref[PLAN_N_TILES:PLAN_N_TILES + 1, :] = jnp.broadcast_to(n_tiles, (1, LANES))


def _moe_plan(bucket):
    t = bucket.shape[1]
    return pl.pallas_call(
        functools.partial(_moe_plan_kernel, blk=512),
        out_shape=[jax.ShapeDtypeStruct((1, t), jnp.int32),
                   jax.ShapeDtypeStruct((PLAN_ROWS, LANES), jnp.int32)],
        compiler_params=pltpu.CompilerParams(vmem_limit_bytes=VMEM_LIMIT),
        name="moe_plan",
    )(bucket)


def _max_tiles(t):
    return (t + N_BUCKETS * (MOE_TILE - 1)) // MOE_TILE


def _row_copy(src, src_row, dst, dst_row, sem):
    return pltpu.make_async_copy(src.at[pl.ds(src_row, 1)], dst.at[pl.ds(dst_row, 1)], sem)


def _moe_scatter_kernel(pos_ref, x0_ref, x1_ref, tail_ref, buf_ref, out_ref, aug_ref, sem, *, n0):
    del buf_ref
    i = pl.program_id(0)
    tm, d = x0_ref.shape
    aug_ref[:, :d] = jnp.where(i < n0, x0_ref[...], x1_ref[...])
    aug_ref[:, d:] = tail_ref[...]

    def issue(r, carry):
        _row_copy(aug_ref, r, out_ref, pos_ref[0, r], sem).start()
        return carry

    lax.fori_loop(0, tm, issue, 0, unroll=8)
    pltpu.make_async_copy(aug_ref, out_ref.at[pl.ds(0, tm)], sem).wait()


def _moe_scatter(xs, streams, tail, pos3, tm):
    d = xs[0].shape[1]
    t = tail.shape[0]
    n0 = streams[0].rows // tm
    buf = jnp.zeros((_max_tiles(t) * MOE_TILE, d + TAIL), F32)
    return pl.pallas_call(
        functools.partial(_moe_scatter_kernel, n0=n0),
        grid=(t // tm,),
        in_specs=[pl.BlockSpec((None, 1, tm), lambda i: (i, 0, 0), memory_space=pltpu.SMEM),
                  *_two_stream_specs(streams, tm, d),
                  pl.BlockSpec((tm, TAIL), lambda i: (i, 0)),
                  pl.BlockSpec(memory_space=pl.ANY)],
        out_specs=pl.BlockSpec(memory_space=pl.ANY),
        out_shape=jax.ShapeDtypeStruct(buf.shape, F32),
        input_output_aliases={4: 0},
        scratch_shapes=[pltpu.VMEM((tm, d + TAIL), F32), pltpu.SemaphoreType.DMA(())],
        compiler_params=_params("arbitrary"),
        name="moe_scatter",
    )(pos3, *xs, tail, buf)


def _moe_expert_kernel(plan_ref, xa_ref, g_ref, mt_ref, wg0_ref, wu0_ref, wd0_ref, wg1_ref, wu1_ref, wd1_ref,
                       o_ref):
    i = pl.program_id(0)
    n_tiles = plan_ref[PLAN_N_TILES * LANES]
    d = o_ref.shape[1]

    @pl.when(i < n_tiles)
    def _():
        x = xa_ref[:, :d]
        gate_lo = xa_ref[:, d:d + 1]
        gate_hi = xa_ref[:, d + 1:d + 2]
        mrow = xa_ref[:, d + 2:d + 3]

        def pick(kind):
            base = 3 * kind
            return jnp.where(mrow < 0.5, mt_ref[base:base + 1, :],
                             jnp.where(mrow < 1.5, mt_ref[base + 1:base + 2, :], mt_ref[base + 2:base + 3, :]))

        h = _adaln(x, g_ref[...], pick(0), pick(1)).astype(BF16)
        out = None
        for gate, wg_ref, wu_ref, wd_ref in ((gate_lo, wg0_ref, wu0_ref, wd0_ref),
                                             (gate_hi, wg1_ref, wu1_ref, wd1_ref)):
            hg = jnp.dot(h, wg_ref[...], preferred_element_type=F32)
            hu = jnp.dot(h, wu_ref[...], preferred_element_type=F32)
            act = (_silu(hg) * hu * gate).astype(BF16)
            y = jnp.dot(act, wd_ref[...], preferred_element_type=F32)
            out = y if out is None else out + y
        o_ref[...] = x + pick(2) * out

    @pl.when(i >= n_tiles)
    def _():
        o_ref[...] = jnp.zeros_like(o_ref)


def _moe_experts(xa, plan, g, mtab, wg, wu, wd):
    rows, da = xa.shape
    d = da - TAIL
    f = wg.shape[2]
    nt = rows // MOE_TILE

    def tile(i, plan_ref):
        return jnp.minimum(i, plan_ref[PLAN_N_TILES * LANES] - 1)

    def wspec(shape, row):
        return pl.BlockSpec((None, *shape), lambda i, p: (p[row * LANES + tile(i, p)], 0, 0))

    grid_spec = pltpu.PrefetchScalarGridSpec(
        num_scalar_prefetch=1,
        grid=(nt,),
        in_specs=[pl.BlockSpec((MOE_TILE, da), lambda i, p: (tile(i, p), 0)),
                  pl.BlockSpec((1, d), lambda i, p: (0, 0)),
                  pl.BlockSpec(mtab.shape, lambda i, p: (0, 0)),
                  wspec((d, f), PLAN_E_LO), wspec((d, f), PLAN_E_LO), wspec((f, d), PLAN_E_LO),
                  wspec((d, f), PLAN_E_HI), wspec((d, f), PLAN_E_HI), wspec((f, d), PLAN_E_HI)],
        out_specs=pl.BlockSpec((MOE_TILE, d), lambda i, p: (i, 0)))
    return pl.pallas_call(
        _moe_expert_kernel,
        grid_spec=grid_spec,
        out_shape=jax.ShapeDtypeStruct((rows, d), F32),
        compiler_params=_params("arbitrary"),
        name="moe_experts",
    )(plan.reshape(-1), xa, g, mtab, wg, wu, wd, wg, wu, wd)


def _moe_gather_kernel(pos_ref, ys_ref, o_ref, sem):
    tm = o_ref.shape[0]

    def issue(r, carry):
        _row_copy(ys_ref, pos_ref[0, r], o_ref, r, sem).start()
        return carry

    lax.fori_loop(0, tm, issue, 0, unroll=8)
    pltpu.make_async_copy(ys_ref.at[pl.ds(0, tm)], o_ref, sem).wait()


def _moe_gather(ys, pos3, tile0, rows, tm):
    d = ys.shape[1]
    return pl.pallas_call(
        _moe_gather_kernel,
        grid=(rows // tm,),
        in_specs=[pl.BlockSpec((None, 1, tm), lambda i: (i + tile0, 0, 0), memory_space=pltpu.SMEM),
                  pl.BlockSpec(memory_space=pl.ANY)],
        out_specs=pl.BlockSpec((tm, d), lambda i: (i, 0)),
        out_shape=jax.ShapeDtypeStruct((rows, d), F32),
        scratch_shapes=[pltpu.SemaphoreType.DMA(())],
        compiler_params=_params("arbitrary"),
        name="moe_gather",
    )(pos3, ys)


def _moe(xs, streams, g, mod, layer, rw_t, rb_col, wg, wu, wd, tm=512):
    tail, bucket = _moe_route(xs, streams, g, mod, layer, rw_t, rb_col)
    pos, plan = _moe_plan(bucket)
    pos3 = pos.reshape(-1, 1, tm)
    xa = _moe_scatter(xs, streams, tail, pos3, tm)
    mtab = mod[layer, 3:6, :, 0, :].reshape(-1, xs[0].shape[1])
    ys = _moe_experts(xa, plan, g, mtab, wg, wu, wd)
    out, tile0 = [], 0
    for st in streams:
        out.append(_moe_gather(ys, pos3, tile0, st.rows, tm))
        tile0 += st.rows // tm
    return out


def _rope_tables(t_len):
    rows = t_len // GRID_W
    row_ids = jnp.repeat(jnp.arange(rows), GRID_W).astype(F32)
    col_ids = jnp.tile(jnp.arange(GRID_W), rows).astype(F32)
    n_freq = HEAD_DIM // 4
    inv = 1.0 / (ROPE_THETA ** (jnp.arange(n_freq, dtype=F32) / n_freq))
    ang = jnp.concatenate([row_ids[:, None] * inv, col_ids[:, None] * inv], axis=-1)
    cos, sin = jnp.cos(ang), jnp.sin(ang)
    cs = jnp.tile(jnp.concatenate([cos, cos], axis=-1), (1, LANES // HEAD_DIM))
    sn = jnp.tile(jnp.concatenate([-sin, sin], axis=-1), (1, LANES // HEAD_DIM))
    return cs, sn


def _pad_lanes(v):
    return jnp.pad(v.reshape(1, -1), ((0, 0), (0, LANES - v.size)))


def kernel(x_prompt, x_sample, c, c_ctx, cache_k, cache_v, state_ssm, mod_w, mod_b, norm1_g, norm2_g,
           attn_w_qkv, attn_q_norm_g, attn_k_norm_g, attn_w_o, ssd_in_proj, ssd_conv_w, ssd_conv_b,
           ssd_dt_bias, ssd_a_log, ssd_d, ssd_norm_g, ssd_out_proj, router_w, router_b, expert_w_gate,
           expert_w_up, expert_w_down):
    n_p, l_p, d = x_prompt.shape
    n_s, l_s, _ = x_sample.shape
    depth = mod_w.shape[0]
    past = cache_k.shape[2]
    nkv = N_KV_HEADS * HEAD_DIM
    assert d == D_MODEL and 1 + n_s == 3
    streams = (_Stream(n_p, l_p, 0, 0), _Stream(n_s, l_s, 1, 1))
    xs = [x_prompt.reshape(n_p * l_p, d), x_sample.reshape(n_s * l_s, d)]

    cond8 = jnp.concatenate([c_ctx[None], c, jnp.zeros((8 - 1 - n_s, d), F32)], axis=0)
    mod = _modulation(cond8, mod_w, mod_b)
    mod = mod[:, :1 + n_s].reshape(depth, 1 + n_s, 6, 1, d).transpose(0, 2, 1, 3, 4)

    rope = _rope_tables(l_s)
    rw_t = jnp.pad(router_w.T, ((0, LANES - N_EXPERTS), (0, 0))).astype(BF16)
    rb_col = jnp.pad(router_b, (0, LANES - N_EXPERTS)).reshape(LANES, 1)
    wg_all, wu_all, wd_all = (w.astype(BF16) for w in (expert_w_gate, expert_w_up, expert_w_down))

    new_k, new_v, new_ssm = [], [], []
    for layer in range(depth):
        i = layer // 2
        g1 = norm1_g[layer][None]
        if layer % 2 == 0:
            w_qkv = attn_w_qkv[i].astype(BF16)
            w_o = attn_w_o[i].astype(BF16)
            qg2 = jnp.tile(attn_q_norm_g[i], LANES // HEAD_DIM)[None]
            kg2 = jnp.tile(attn_k_norm_g[i], LANES // HEAD_DIM)[None]
            for si, st in enumerate(streams):
                qkv = _ada_mm(xs[si], g1, mod, layer, st, w_qkv, tn=w_qkv.shape[1])
                if si == 0:
                    q, k2, v2, kc, vc = _qk_prep(qkv, qg2, kg2, emit_cache=True)
                    o = _attention(q, k2, v2, st.n_seq, st.seq_len)
                    new_k.append(kc.reshape(n_p, l_p, N_KV_HEADS, HEAD_DIM))
                    new_v.append(vc.reshape(n_p, l_p, N_KV_HEADS, HEAD_DIM))
                else:
                    q, k2, v2 = _qk_prep(qkv, qg2, kg2, tables=rope)
                    kc2, vc2 = _kv_expand(cache_k[:, i].reshape(n_s * past, nkv),
                                          cache_v[:, i].reshape(n_s * past, nkv))
                    o = _attention(q, k2, v2, st.n_seq, st.seq_len, ctx=(kc2, vc2, past))
                xs[si] = _mm_res(o, w_o, xs[si], mod, layer, st)
        else:
            w_in = ssd_in_proj[i]
            nzx = D_INNER + CONV_DIM
            w_zx = w_in[:, :nzx].astype(BF16)
            w_dt = jnp.pad(w_in[:, nzx:], ((0, 0), (0, LANES - 2 * SSD_HEADS))).astype(BF16)
            w_out = ssd_out_proj[i].astype(BF16)
            dtb = _pad_lanes(ssd_dt_bias[i])
            alog = _pad_lanes(ssd_a_log[i])
            dskip = jnp.repeat(ssd_d[i], SSD_HEAD_DIM)[None]
            cw, cb = ssd_conv_w[i], ssd_conv_b[i][None]
            for si, st in enumerate(streams):
                zx, dt = _ada_mm(xs[si], g1, mod, layer, st, w_zx, tn=1024, w_side=w_dt)
                xbc = _conv_silu(zx, D_INNER, st, cw, cb, tc=1024 if si == 0 else 256)
                init = None if si == 0 else state_ssm[:, i].reshape(n_s, 2, D_INNER, D_STATE)
                y_f, s_f = _ssd_scan(xbc, dt, dtb, alog, st, rev=False, init=init)
                yg, s_b = _ssd_scan(xbc, dt, dtb, alog, st, rev=True, init=init, final=(y_f, zx, dskip))
                if si == 0:
                    new_ssm.append(jnp.stack([s_f, s_b], axis=1).reshape(
                        n_p, 2, SSD_HEADS, SSD_HEAD_DIM, D_STATE))
                xs[si] = _mm_res(yg, w_out, xs[si], mod, layer, st, norm_g=ssd_norm_g[i][None])
        xs = _moe(xs, streams, norm2_g[layer][None], mod, layer, rw_t, rb_col,
                  wg_all[layer], wu_all[layer], wd_all[layer])

    return (xs[0].reshape(n_p, l_p, d), xs[1].reshape(n_s, l_s, d),
            jnp.stack(new_k, axis=1), jnp.stack(new_v, axis=1), jnp.stack(new_ssm, axis=1))
```

```python
import functools
import math
from typing import NamedTuple

import jax
import jax.numpy as jnp
from jax import lax
from jax.experimental import pallas as pl
from jax.experimental.pallas import tpu as pltpu

F32 = jnp.float32
BF16 = jnp.bfloat16

D_MODEL = 1024
GRID_W = 64
N_HEADS = 16
N_KV_HEADS = 4
HEAD_DIM = 64
KV_REP = N_HEADS // N_KV_HEADS
ROPE_THETA = 10000.0
D_INNER = 2048
SSD_HEAD_DIM = 64
SSD_HEADS = 32
SSD_GROUPS = 4
SSD_REP = SSD_HEADS // SSD_GROUPS
D_STATE = 128
D_CONV = 5
CHUNK = 128
CONV_DIM = D_INNER + 2 * SSD_GROUPS * D_STATE
N_EXPERTS = 16
N_EXPERT_GROUPS = 4
EXPERTS_PER_GROUP = 4
D_FF_EXPERT = 256
EPS = 1e-6

LANES = 128
VMEM_LIMIT = 56 * 1024 * 1024


def _params(*sem):
    return pltpu.CompilerParams(dimension_semantics=sem, vmem_limit_bytes=VMEM_LIMIT)


def _silu(x):
    return x * jax.nn.sigmoid(x)


class _Stream(NamedTuple):
    n_seq: int
    seq_len: int
    mod_base: int
    mod_step: int

    @property
    def rows(self):
        return self.n_seq * self.seq_len


def _mod_kernel(c_ref, w_ref, b_ref, o_ref):
    s = _silu(c_ref[...])
    o_ref[...] = jnp.dot(s.astype(BF16), w_ref[...].astype(BF16),
                         preferred_element_type=F32) + b_ref[...]


def _modulation(cond8, mod_w, mod_b):
    depth, d, n = mod_w.shape
    tn = 1536
    return pl.pallas_call(
        _mod_kernel,
        grid=(depth, n // tn),
        in_specs=[pl.BlockSpec((8, d), lambda l, j: (0, 0)),
                  pl.BlockSpec((None, d, tn), lambda l, j: (l, 0, j)),
                  pl.BlockSpec((None, 1, tn), lambda l, j: (l, 0, j))],
        out_specs=pl.BlockSpec((None, 8, tn), lambda l, j: (l, 0, j)),
        out_shape=jax.ShapeDtypeStruct((depth, 8, n), F32),
        compiler_params=_params("parallel", "parallel"),
        name="modulation",
    )(cond8, mod_w, mod_b.reshape(depth, 1, n))


def _mod_spec(layer, chunk, tm, st):
    def imap(i, *_):
        return (layer, chunk, st.mod_base + st.mod_step * (i * tm // st.seq_len), 0, 0)
    return pl.BlockSpec((None, None, None, 1, D_MODEL), imap)


def _adaln(x, g, sh, sc):
    ms = jnp.mean(x * x, axis=-1, keepdims=True)
    y = x * lax.rsqrt(ms + EPS) * g
    return y * (1 + sc) + sh


def _ada_mm_kernel(x_ref, g_ref, sh_ref, sc_ref, w_ref, o_ref):
    h = _adaln(x_ref[...], g_ref[...], sh_ref[...], sc_ref[...]).astype(BF16)
    o_ref[...] = jnp.dot(h, w_ref[...], preferred_element_type=F32)


def _ada_mm(x, g, mod, layer, st, w, tm=512):
    t, d = x.shape
    n = w.shape[1]
    return pl.pallas_call(
        _ada_mm_kernel,
        grid=(t // tm,),
        in_specs=[pl.BlockSpec((tm, d), lambda i: (i, 0)),
                  pl.BlockSpec((1, d), lambda i: (0, 0)),
                  _mod_spec(layer, 0, tm, st),
                  _mod_spec(layer, 1, tm, st),
                  pl.BlockSpec((d, n), lambda i: (0, 0))],
        out_specs=pl.BlockSpec((tm, n), lambda i: (i, 0)),
        out_shape=jax.ShapeDtypeStruct((t, n), F32),
        compiler_params=_params("parallel"),
        name="ada_mm",
    )(x, g, mod, mod, w)


def _mm_res_kernel(a_ref, w_ref, r_ref, gate_ref, o_ref):
    y = jnp.dot(a_ref[...], w_ref[...], preferred_element_type=F32)
    o_ref[...] = r_ref[...] + gate_ref[...] * y


def _norm_mm_res_kernel(a_ref, ng_ref, w_ref, r_ref, gate_ref, o_ref):
    a = a_ref[...]
    ms = jnp.mean(a * a, axis=-1, keepdims=True)
    an = (a * lax.rsqrt(ms + EPS) * ng_ref[...]).astype(BF16)
    y = jnp.dot(an, w_ref[...], preferred_element_type=F32)
    o_ref[...] = r_ref[...] + gate_ref[...] * y


def _mm_res(a, w, res, mod, layer, st, norm_g=None, tm=512):
    t, k = a.shape
    n = w.shape[1]
    a_spec = pl.BlockSpec((tm, k), lambda i: (i, 0))
    w_spec = pl.BlockSpec((k, n), lambda i: (0, 0))
    r_spec = pl.BlockSpec((tm, n), lambda i: (i, 0))
    gate_spec = _mod_spec(layer, 2, tm, st)
    if norm_g is None:
        kern, specs, args = _mm_res_kernel, [a_spec, w_spec, r_spec, gate_spec], (a, w, res, mod)
    else:
        kern = _norm_mm_res_kernel
        specs = [a_spec, pl.BlockSpec((1, k), lambda i: (0, 0)), w_spec, r_spec, gate_spec]
        args = (a, norm_g, w, res, mod)
    return pl.pallas_call(
        kern,
        grid=(t // tm,),
        in_specs=specs,
        out_specs=pl.BlockSpec((tm, n), lambda i: (i, 0)),
        out_shape=jax.ShapeDtypeStruct((t, n), F32),
        compiler_params=_params("parallel"),
        name="mm_res",
    )(*args)


def _lane_iota(rows):
    return lax.broadcasted_iota(jnp.int32, (rows, LANES), 1)


def _head_pair_norm(c, g2):
    lo = _lane_iota(c.shape[0]) < HEAD_DIM
    cc = c * c
    s_lo = jnp.sum(jnp.where(lo, cc, 0.0), axis=-1, keepdims=True)
    s_hi = jnp.sum(jnp.where(lo, 0.0, cc), axis=-1, keepdims=True)
    r = jnp.where(lo, lax.rsqrt(s_lo / HEAD_DIM + EPS), lax.rsqrt(s_hi / HEAD_DIM + EPS))
    return c * r * g2


def _rope_pair(y, cs, sn):
    first = (_lane_iota(y.shape[0]) % HEAD_DIM) < HEAD_DIM // 2
    partner = jnp.where(first, pltpu.roll(y, LANES - HEAD_DIM // 2, 1), pltpu.roll(y, HEAD_DIM // 2, 1))
    return y * cs + partner * sn


def _store_v_split(ref, j, c):
    lane = _lane_iota(c.shape[0])
    lo = lane < HEAD_DIM
    a_lo = jnp.where(lo, c, jnp.where(lane == HEAD_DIM, 1.0, 0.0))
    b_hi = jnp.where(lo, jnp.where(lane == 0, 1.0, 0.0), c)
    ref[2 * j, 0] = a_lo.astype(BF16)
    ref[2 * j, 1] = pltpu.roll(a_lo, HEAD_DIM, 1).astype(BF16)
    ref[2 * j + 1, 0] = pltpu.roll(b_hi, HEAD_DIM, 1).astype(BF16)
    ref[2 * j + 1, 1] = b_hi.astype(BF16)


def _store_kt_split(ref, j, c):
    t = c.T
    a, b = t[:HEAD_DIM], t[HEAD_DIM:]
    z = jnp.zeros_like(a)
    ref[2 * j, 0] = jnp.concatenate([a, z], axis=0).astype(BF16)
    ref[2 * j, 1] = jnp.concatenate([z, a], axis=0).astype(BF16)
    ref[2 * j + 1, 0] = jnp.concatenate([b, z], axis=0).astype(BF16)
    ref[2 * j + 1, 1] = jnp.concatenate([z, b], axis=0).astype(BF16)


def _qk_prep_kernel(*refs, rope, emit_cache):
    it = iter(refs)
    qkv_ref, qg_ref, kg_ref = next(it), next(it), next(it)
    cs_ref, sn_ref = (next(it), next(it)) if rope else (None, None)
    q_ref, k2_ref, v2_ref = next(it), next(it), next(it)
    kc_ref, vc_ref = (next(it), next(it)) if emit_cache else (None, None)

    nq = N_HEADS * HEAD_DIM
    nkv = N_KV_HEADS * HEAD_DIM
    scale = HEAD_DIM ** -0.5 * math.log2(math.e)
    for j in range(nq // LANES):
        y = _head_pair_norm(qkv_ref[:, j * LANES:(j + 1) * LANES], qg_ref[...])
        if rope:
            y = _rope_pair(y, cs_ref[...], sn_ref[...])
        q_ref[:, j * LANES:(j + 1) * LANES] = (y * scale).astype(BF16)
    for j in range(nkv // LANES):
        y = _head_pair_norm(qkv_ref[:, nq + j * LANES:nq + (j + 1) * LANES], kg_ref[...])
        if rope:
            y = _rope_pair(y, cs_ref[...], sn_ref[...])
        if emit_cache:
            kc_ref[:, j * LANES:(j + 1) * LANES] = y
        _store_kt_split(k2_ref, j, y)
        v = qkv_ref[:, nq + nkv + j * LANES:nq + nkv + (j + 1) * LANES]
        if emit_cache:
            vc_ref[:, j * LANES:(j + 1) * LANES] = v
        _store_v_split(v2_ref, j, v)


def _kv_shapes(rows):
    return (jax.ShapeDtypeStruct((N_KV_HEADS, 2, LANES, rows), BF16),
            jax.ShapeDtypeStruct((N_KV_HEADS, 2, rows, LANES), BF16))


def _kv_specs(tm):
    return (pl.BlockSpec((N_KV_HEADS, 2, LANES, tm), lambda i: (0, 0, 0, i)),
            pl.BlockSpec((N_KV_HEADS, 2, tm, LANES), lambda i: (0, 0, i, 0)))


def _qk_prep(qkv, qg2, kg2, tables=None, emit_cache=False, tm=512):
    rows, n = qkv.shape
    nq = N_HEADS * HEAD_DIM
    nkv = N_KV_HEADS * HEAD_DIM
    rope = tables is not None
    in_specs = [pl.BlockSpec((tm, n), lambda i: (i, 0)),
                pl.BlockSpec((1, LANES), lambda i: (0, 0)),
                pl.BlockSpec((1, LANES), lambda i: (0, 0))]
    args = [qkv, qg2, kg2]
    if rope:
        nt = tables[0].shape[0] // tm
        in_specs += [pl.BlockSpec((tm, LANES), lambda i: (i % nt, 0))] * 2
        args += list(tables)
    out_shape = [jax.ShapeDtypeStruct((rows, nq), BF16), *_kv_shapes(rows)]
    out_specs = [pl.BlockSpec((tm, nq), lambda i: (i, 0)), *_kv_specs(tm)]
    if emit_cache:
        out_shape += [jax.ShapeDtypeStruct((rows, nkv), F32)] * 2
        out_specs += [pl.BlockSpec((tm, nkv), lambda i: (i, 0))] * 2
    return pl.pallas_call(
        functools.partial(_qk_prep_kernel, rope=rope, emit_cache=emit_cache),
        grid=(rows // tm,),
        in_specs=in_specs,
        out_specs=out_specs,
        out_shape=out_shape,
        compiler_params=_params("parallel"),
        name="qk_prep",
    )(*args)


def _kv_expand_kernel(k_ref, v_ref, k2_ref, v2_ref):
    for j in range(N_KV_HEADS * HEAD_DIM // LANES):
        _store_kt_split(k2_ref, j, k_ref[:, j * LANES:(j + 1) * LANES])
        _store_v_split(v2_ref, j, v_ref[:, j * LANES:(j + 1) * LANES])


def _kv_expand(k, v, tm=512):
    rows, nkv = k.shape
    return pl.pallas_call(
        _kv_expand_kernel,
        grid=(rows // tm,),
        in_specs=[pl.BlockSpec((tm, nkv), lambda i: (i, 0))] * 2,
        out_specs=list(_kv_specs(tm)),
        out_shape=list(_kv_shapes(rows)),
        compiler_params=_params("parallel"),
        name="kv_expand",
    )(k, v)


def _attn_kernel(*refs, tk, n_tiles, has_ctx):
    if has_ctx:
        q_ref, k_ref, v_ref, kc_ref, vc_ref, o_ref = refs
    else:
        q_ref, k_ref, v_ref, o_ref = refs
        kc_ref = vc_ref = None
    tq = q_ref.shape[0]
    lo = _lane_iota(tq) < HEAD_DIM
    tiles = [(k_ref, v_ref, t) for t in range(n_tiles)]
    if has_ctx:
        tiles += [(kc_ref, vc_ref, t) for t in range(vc_ref.shape[1] // tk)]
    for pair in range(KV_REP // 2):
        qp = q_ref[:, pair * LANES:(pair + 1) * LANES]
        m = [jnp.full((tq, 1), -jnp.inf, F32)] * 2
        acc = [jnp.zeros((tq, LANES), F32)] * 2
        for kr, vr, t in tiles:
            for half in range(2):
                s = jnp.dot(qp, kr[half, :, t * tk:(t + 1) * tk], preferred_element_type=F32)
                m_new = jnp.maximum(m[half], jnp.max(s, axis=-1, keepdims=True))
                p = jnp.exp2(s - m_new).astype(BF16)
                acc[half] = jnp.exp2(m[half] - m_new) * acc[half] + jnp.dot(
                    p, vr[half, t * tk:(t + 1) * tk, :], preferred_element_type=F32)
                m[half] = m_new
        out_lo = acc[0] / acc[0][:, HEAD_DIM:HEAD_DIM + 1]
        out_hi = acc[1] / acc[1][:, 0:1]
        o_ref[:, pair * LANES:(pair + 1) * LANES] = jnp.where(lo, out_lo, out_hi).astype(BF16)


def _attention(q, k2, v2, n_batch, t_len, ctx=None, tq=512, tk=512):
    tk = min(tk, t_len)
    tq = min(tq, t_len)
    nq = t_len // tq
    gw = KV_REP * HEAD_DIM
    q_spec = pl.BlockSpec((tq, gw), lambda b, g, i: (b * nq + i, g))
    in_specs = [q_spec,
                pl.BlockSpec((None, 2, LANES, t_len), lambda b, g, i: (g, 0, 0, b)),
                pl.BlockSpec((None, 2, t_len, LANES), lambda b, g, i: (g, 0, b, 0))]
    args = [q, k2, v2]
    if ctx is not None:
        kc2, vc2, ctx_len = ctx
        in_specs += [pl.BlockSpec((None, 2, LANES, ctx_len), lambda b, g, i: (g, 0, 0, b)),
                     pl.BlockSpec((None, 2, ctx_len, LANES), lambda b, g, i: (g, 0, b, 0))]
        args += [kc2, vc2]
    return pl.pallas_call(
        functools.partial(_attn_kernel, tk=tk, n_tiles=t_len // tk, has_ctx=ctx is not None),
        grid=(n_batch, N_KV_HEADS, nq),
        in_specs=in_specs,
        out_specs=q_spec,
        out_shape=jax.ShapeDtypeStruct(q.shape, BF16),
        compiler_params=_params("parallel", "parallel", "arbitrary"),
        name="attention",
    )(*args)


HALO = 16


def _in_proj_kernel(x_ref, xa_ref, xb_ref, g_ref, sh_ref, sc_ref, w_ref, wdt_ref, cw_ref, cb_ref,
                    z_ref, xbc_ref, dt_ref, h_ref, u_ref, *, n_z_tiles, seq_len):
    i, j = pl.program_id(0), pl.program_id(1)
    tm = x_ref.shape[0]

    @pl.when(j == 0)
    def _():
        def norm(ref):
            return _adaln(ref[...], g_ref[...], sh_ref[...], sc_ref[...]).astype(BF16)
        h_ref[0:HALO] = norm(xa_ref)
        h = norm(x_ref)
        h_ref[HALO:HALO + tm] = h
        h_ref[HALO + tm:] = norm(xb_ref)
        dt_ref[...] = jnp.dot(h, wdt_ref[...], preferred_element_type=F32)

    @pl.when(j < n_z_tiles)
    def _():
        z_ref[...] = jnp.dot(h_ref[HALO:HALO + tm], w_ref[...], preferred_element_type=F32)

    @pl.when(j >= n_z_tiles)
    def _():
        u = jnp.dot(h_ref[...], w_ref[...], preferred_element_type=F32)
        keep_a = ((i * tm) % seq_len != 0).astype(F32)
        keep_b = (((i + 1) * tm) % seq_len != 0).astype(F32)
        u_ref[0:HALO] = u[0:HALO] * keep_a
        u_ref[HALO:HALO + tm] = u[HALO:HALO + tm]
        u_ref[HALO + tm:] = u[HALO + tm:] * keep_b
        acc = cb_ref[...]
        for k in range(D_CONV):
            d = k - D_CONV // 2
            acc = acc + u_ref[HALO + d:HALO + d + tm, :] * cw_ref[k:k + 1, :]
        xbc_ref[...] = _silu(acc)


def _in_proj_conv(x, g, mod, layer, st, w_zx, w_dt, conv_w, conv_b, tm=512, tn=1024):
    t, d = x.shape
    tm = min(tm, st.seq_len)
    assert st.seq_len % tm == 0 and D_INNER % tn == 0 and CONV_DIM % tn == 0
    nzt, nct = D_INNER // tn, CONV_DIM // tn
    hb = tm // HALO
    last_halo = t // HALO - 1
    return pl.pallas_call(
        functools.partial(_in_proj_kernel, n_z_tiles=nzt, seq_len=st.seq_len),
        grid=(t // tm, nzt + nct),
        in_specs=[pl.BlockSpec((tm, d), lambda i, j: (i, 0)),
                  pl.BlockSpec((HALO, d), lambda i, j: (jnp.maximum(i * hb - 1, 0), 0)),
                  pl.BlockSpec((HALO, d), lambda i, j: (jnp.minimum((i + 1) * hb, last_halo), 0)),
                  pl.BlockSpec((1, d), lambda i, j: (0, 0)),
                  _mod_spec(layer, 0, tm, st),
                  _mod_spec(layer, 1, tm, st),
                  pl.BlockSpec((d, tn), lambda i, j: (0, j)),
                  pl.BlockSpec((d, LANES), lambda i, j: (0, 0)),
                  pl.BlockSpec((D_CONV, tn), lambda i, j: (0, jnp.maximum(j - nzt, 0))),
                  pl.BlockSpec((1, tn), lambda i, j: (0, jnp.maximum(j - nzt, 0)))],
        out_specs=[pl.BlockSpec((tm, tn), lambda i, j: (i, jnp.minimum(j, nzt - 1))),
                   pl.BlockSpec((tm, tn), lambda i, j: (i, jnp.maximum(j - nzt, 0))),
                   pl.BlockSpec((tm, LANES), lambda i, j: (i, 0))],
        out_shape=[jax.ShapeDtypeStruct((t, D_INNER), F32),
                   jax.ShapeDtypeStruct((t, CONV_DIM), F32),
                   jax.ShapeDtypeStruct((t, LANES), F32)],
        scratch_shapes=[pltpu.VMEM((tm + 2 * HALO, d), BF16), pltpu.VMEM((tm + 2 * HALO, tn), F32)],
        compiler_params=_params("parallel", "arbitrary"),
        name="in_proj_conv",
    )(x, x, x, g, mod, mod, w_zx, w_dt, conv_w, conv_b)


def _cumsum_rows(v, reverse):
    n = v.shape[0]
    row = lax.broadcasted_iota(jnp.int32, v.shape, 0)
    k = 1
    while k < n:
        if reverse:
            v = v + jnp.where(row < n - k, pltpu.roll(v, n - k, 0), 0.0)
        else:
            v = v + jnp.where(row >= k, pltpu.roll(v, k, 0), 0.0)
        k *= 2
    return v


def _ssd_kernel(*refs, rev, final_pass, has_init, cps):
    it = iter(refs)
    x_ref, b_ref, c_ref, dt_ref, dtb_ref, alog_ref = [next(it) for _ in range(6)]
    init_ref = next(it) if has_init else None
    if final_pass:
        y0_ref, z_ref, dskip_ref = next(it), next(it), next(it)
    y_ref, fin_ref, st_ref = next(it), next(it), next(it)

    ci = pl.program_id(1)
    n_blk = D_INNER // LANES

    @pl.when(ci == 0)
    def _():
        if has_init:
            for k in range(n_blk):
                st_ref[:, k * LANES:(k + 1) * LANES] = init_ref[k * LANES:(k + 1) * LANES, :].T
        else:
            st_ref[...] = jnp.zeros_like(st_ref)

    dirn = 1 if rev else 0
    neg_a = -jnp.exp(alog_ref[...])
    row = lax.broadcasted_iota(jnp.int32, (CHUNK, CHUNK), 0)
    col = lax.broadcasted_iota(jnp.int32, (CHUNK, CHUNK), 1)
    causal = (col >= row) if rev else (col <= row)
    lo = _lane_iota(CHUNK) < SSD_HEAD_DIM
    last = 0 if rev else CHUNK - 1

    for cc in (reversed(range(cps)) if rev else range(cps)):
        rows = slice(cc * CHUNK, (cc + 1) * CHUNK)
        dtv = jax.nn.softplus(dt_ref[rows, :] + dtb_ref[...])
        acum = _cumsum_rows(dtv * neg_a, rev) * math.log2(math.e)
        acum_t = acum.T
        dtv_t = dtv.T
        w_t = dtv_t * jnp.exp2(acum_t[:, last:last + 1] - acum_t)
        e_tot = jnp.exp2(acum[last:last + 1, :])
        src_t = acum_t - jnp.log2(dtv_t)
        for g in range(SSD_GROUPS):
            bg = b_ref[rows, g * D_STATE:(g + 1) * D_STATE]
            cg = c_ref[rows, g * D_STATE:(g + 1) * D_STATE]
            cb = lax.dot_general(cg.astype(BF16), bg.astype(BF16), (((1,), (1,)), ((), ())),
                                 preferred_element_type=F32)
            bg_t = bg.T
            for pp in range(SSD_REP // 2):
                h0 = g * SSD_REP + 2 * pp
                i0 = dirn * SSD_HEADS + h0
                cols = slice(h0 * SSD_HEAD_DIM, (h0 + 2) * SSD_HEAD_DIM)
                xp = x_ref[rows, cols]
                st_pair = st_ref[:, cols]
                halves = ((jnp.where(lo, xp, 0.0).astype(BF16), jnp.where(lo, st_pair, 0.0).astype(BF16)),
                          (jnp.where(lo, 0.0, xp).astype(BF16), jnp.where(lo, 0.0, st_pair).astype(BF16)))
                y = new = None
                for hh, (xh, sth) in enumerate(halves):
                    i = i0 + hh
                    a_col = jnp.broadcast_to(acum[:, i:i + 1], (CHUNK, CHUNK))
                    lmat = jnp.exp2(jnp.where(causal, a_col - src_t[i:i + 1, :], -jnp.inf))
                    m = (cb * lmat).astype(BF16)
                    ce = (cg * jnp.exp2(a_col)).astype(BF16)
                    yh = jnp.dot(jnp.concatenate([m, ce], axis=1), jnp.concatenate([xh, sth], axis=0),
                                 preferred_element_type=F32)
                    nh = jnp.dot((bg_t * w_t[i:i + 1, :]).astype(BF16), xh, preferred_element_type=F32)
                    y = yh if y is None else y + yh
                    new = nh if new is None else new + nh
                dec = jnp.where(lo[:1], e_tot[:, i0:i0 + 1], e_tot[:, i0 + 1:i0 + 2])
                st_ref[:, cols] = st_pair * dec + new
                if final_pass:
                    y = (y0_ref[rows, cols] + y + xp * dskip_ref[:, cols]) * _silu(z_ref[rows, cols])
                y_ref[rows, cols] = y

    @pl.when(ci == pl.num_programs(1) - 1)
    def _():
        for k in range(n_blk):
            fin_ref[k * LANES:(k + 1) * LANES, :] = st_ref[:, k * LANES:(k + 1) * LANES].T


def _ssd_scan(xbc, dt, dtb, alog, st, rev, init=None, final=None, cps=4):
    cps = min(cps, st.seq_len // CHUNK)
    blk = cps * CHUNK
    nb = st.seq_len // blk
    dirn = 1 if rev else 0

    def tok(s, c):
        return (s * nb + (nb - 1 - c if rev else c), 0)

    bc_w = SSD_GROUPS * D_STATE
    b_col = D_INNER // bc_w
    tok_spec = pl.BlockSpec((blk, D_INNER), tok)
    in_specs = [tok_spec,
                pl.BlockSpec((blk, bc_w), lambda s, c: (tok(s, c)[0], b_col)),
                pl.BlockSpec((blk, bc_w), lambda s, c: (tok(s, c)[0], b_col + 1)),
                pl.BlockSpec((blk, LANES), tok),
                pl.BlockSpec((1, LANES), lambda s, c: (0, 0)),
                pl.BlockSpec((1, LANES), lambda s, c: (0, 0))]
    args = [xbc, xbc, xbc, dt, dtb, alog]
    if init is not None:
        in_specs.append(pl.BlockSpec((None, None, D_INNER, D_STATE), lambda s, c: (s, dirn, 0, 0)))
        args.append(init)
    if final is not None:
        y0, zx, dskip = final
        in_specs += [tok_spec, tok_spec, pl.BlockSpec((1, D_INNER), lambda s, c: (0, 0))]
        args += [y0, zx, dskip]
    return pl.pallas_call(
        functools.partial(_ssd_kernel, rev=rev, final_pass=final is not None,
                          has_init=init is not None, cps=cps),
        grid=(st.n_seq, nb),
        in_specs=in_specs,
        out_specs=[tok_spec, pl.BlockSpec((None, D_INNER, D_STATE), lambda s, c: (s, 0, 0))],
        out_shape=[jax.ShapeDtypeStruct((st.rows, D_INNER), F32),
                   jax.ShapeDtypeStruct((st.n_seq, D_INNER, D_STATE), F32)],
        scratch_shapes=[pltpu.VMEM((D_STATE, D_INNER), F32)],
        compiler_params=_params("parallel", "arbitrary"),
        name="ssd_scan",
    )(*args)


MOE_TILE = 256
PAIRS_PER_GROUP = EXPERTS_PER_GROUP * (EXPERTS_PER_GROUP - 1) // 2
N_BUCKETS = N_EXPERT_GROUPS * PAIRS_PER_GROUP
BUCKET_ROWS = 32
TAIL = LANES


def _route_picks(biased_t, scores_t):
    ng, ne = N_EXPERT_GROUPS, EXPERTS_PER_GROUP
    b = [[biased_t[j * ne + k:j * ne + k + 1, :] for k in range(ne)] for j in range(ng)]
    s = [[scores_t[j * ne + k:j * ne + k + 1, :] for k in range(ne)] for j in range(ng)]
    gs = []
    for j in range(ng):
        best = None
        for k1 in range(ne):
            for k2 in range(k1 + 1, ne):
                ps = b[j][k1] + b[j][k2]
                best = ps if best is None else jnp.maximum(best, ps)
        gs.append(best)
    sel = jnp.zeros_like(gs[0], jnp.int32)
    best = gs[0]
    for j in range(1, ng):
        gt = gs[j] > best
        best = jnp.where(gt, gs[j], best)
        sel = jnp.where(gt, j, sel)
    in_grp = [sel == j for j in range(ng)]

    def pick(vals):
        out = []
        for k in range(ne):
            v = vals[ng - 1][k]
            for j in range(ng - 2, -1, -1):
                v = jnp.where(in_grp[j], vals[j][k], v)
            out.append(v)
        return out

    yb, ys = pick(b), pick(s)

    def argmax_first(vals):
        bv, bi = vals[0], jnp.zeros_like(sel)
        for k in range(1, ne):
            gt = vals[k] > bv
            bv = jnp.where(gt, vals[k], bv)
            bi = jnp.where(gt, k, bi)
        return bi

    i1 = argmax_first(yb)
    i2 = argmax_first([jnp.where(i1 == k, -jnp.inf, yb[k]) for k in range(ne)])

    def take(vals, idx):
        v = vals[ne - 1]
        for k in range(ne - 2, -1, -1):
            v = jnp.where(idx == k, vals[k], v)
        return v

    w1, w2 = take(ys, i1), take(ys, i2)
    tot = w1 + w2
    return sel, i1, i2, w1 / tot, w2 / tot


def _two_stream_specs(streams, tm, width):
    n0 = streams[0].rows // tm
    return (pl.BlockSpec((tm, width), lambda i, *_: (jnp.minimum(i, n0 - 1), 0)),
            pl.BlockSpec((tm, width), lambda i, *_: (jnp.maximum(i - n0, 0), 0)))


def _moe_route_kernel(x0_ref, x1_ref, g_ref, sh0_ref, sc0_ref, sh1_ref, sc1_ref, rw_ref, rb_ref,
                      tail_ref, bucket_ref, tt_ref, *, n0, mod_of_tile):
    i = pl.program_id(0)
    first = i < n0
    x = jnp.where(first, x0_ref[...], x1_ref[...])
    sh = jnp.where(first, sh0_ref[...], sh1_ref[...])
    sc = jnp.where(first, sc0_ref[...], sc1_ref[...])
    h = _adaln(x, g_ref[...], sh, sc).astype(BF16)
    logits_t = lax.dot_general(rw_ref[...], h, (((1,), (1,)), ((), ())),
                               preferred_element_type=F32)
    scores_t = jax.nn.sigmoid(logits_t)
    sel, i1, i2, g1, g2 = _route_picks(scores_t + rb_ref[...], scores_t)
    first_lo = i1 < i2
    klo = jnp.where(first_lo, i1, i2)
    khi = jnp.where(first_lo, i2, i1)
    pair = jnp.where(klo == 0, 0, jnp.where(klo == 1, 3, 5)) + (khi - klo - 1)
    bucket_ref[...] = sel * PAIRS_PER_GROUP + pair
    tt_ref[...] = jnp.zeros_like(tt_ref)
    tt_ref[0:1, :] = jnp.where(first_lo, g1, g2)
    tt_ref[1:2, :] = jnp.where(first_lo, g2, g1)
    tt_ref[2:3, :] = jnp.full(g1.shape, 1.0, F32) * mod_of_tile(i).astype(F32)
    tail_ref[...] = tt_ref[...].T


def _moe_route(xs, streams, g, mod, layer, rw_t, rb_col, tm=1024):
    d = xs[0].shape[1]
    t = sum(st.rows for st in streams)
    n0 = streams[0].rows // tm
    s0, s1 = streams

    def mod_of_tile(i):
        row0 = s0.mod_base + s0.mod_step * (i * tm // s0.seq_len)
        row1 = s1.mod_base + s1.mod_step * ((i - n0) * tm // s1.seq_len)
        return jnp.where(i < n0, row0, row1)

    def mspec(chunk, st, shift):
        def imap(i):
            return (layer, chunk, st.mod_base + st.mod_step * (jnp.maximum(i - shift, 0) * tm // st.seq_len), 0, 0)
        return pl.BlockSpec((None, None, None, 1, d), imap)

    x_specs = _two_stream_specs(streams, tm, d)
    return pl.pallas_call(
        functools.partial(_moe_route_kernel, n0=n0, mod_of_tile=mod_of_tile),
        grid=(t // tm,),
        in_specs=[*x_specs,
                  pl.BlockSpec((1, d), lambda i: (0, 0)),
                  mspec(3, s0, 0), mspec(4, s0, 0), mspec(3, s1, n0), mspec(4, s1, n0),
                  pl.BlockSpec((LANES, d), lambda i: (0, 0)),
                  pl.BlockSpec((LANES, 1), lambda i: (0, 0))],
        out_specs=[pl.BlockSpec((tm, TAIL), lambda i: (i, 0)),
                   pl.BlockSpec((1, tm), lambda i: (0, i))],
        out_shape=[jax.ShapeDtypeStruct((t, TAIL), F32), jax.ShapeDtypeStruct((1, t), jnp.int32)],
        scratch_shapes=[pltpu.VMEM((LANES, tm), F32)],
        compiler_params=_params("parallel"),
        name="moe_route",
    )(*xs, g, mod, mod, mod, mod, rw_t, rb_col)


PLAN_E_LO, PLAN_E_HI, PLAN_N_TILES, PLAN_ROWS = 0, 1, 2, 8


def _moe_plan_kernel(bucket_ref, pos_ref, plan_ref, *, blk):
    n_tok = bucket_ref.shape[1]
    bid = lax.broadcasted_iota(jnp.int32, (BUCKET_ROWS, blk), 0)
    r = lax.broadcasted_iota(jnp.int32, (blk, blk), 0)
    c = lax.broadcasted_iota(jnp.int32, (blk, blk), 1)
    upper = (r <= c).astype(BF16)

    def onehot(k):
        return (bucket_ref[:, k * blk:(k + 1) * blk] == bid).astype(F32)

    counts = jnp.zeros((BUCKET_ROWS, 1), F32)
    for k in range(n_tok // blk):
        counts = counts + jnp.sum(onehot(k), axis=1, keepdims=True)
    padded = jnp.floor((counts + (MOE_TILE - 1)) * (1.0 / MOE_TILE)) * MOE_TILE
    padded_b = jnp.broadcast_to(padded, (BUCKET_ROWS, LANES))
    starts_b = _cumsum_rows(padded_b, False) - padded_b
    starts = starts_b[:, 0:1]

    carry = jnp.zeros((BUCKET_ROWS, 1), F32)
    for k in range(n_tok // blk):
        oh = onehot(k)
        incl = jnp.dot(oh.astype(BF16), upper, preferred_element_type=F32)
        slot = jnp.sum(oh * (starts + carry + incl - 1.0), axis=0, keepdims=True)
        pos_ref[:, k * blk:(k + 1) * blk] = slot.astype(jnp.int32)
        carry = carry + incl[:, blk - 1:blk]

    tile0 = (lax.broadcasted_iota(jnp.int32, (BUCKET_ROWS, LANES), 1) * MOE_TILE).astype(F32)
    inside = (tile0 >= starts_b) & (tile0 < starts_b + padded_b)
    brow = lax.broadcasted_iota(jnp.int32, (BUCKET_ROWS, LANES), 0).astype(F32)
    tb = jnp.sum(jnp.where(inside, brow, 0.0), axis=0, keepdims=True).astype(jnp.int32)
    n_tiles = jnp.sum(jnp.sum(inside.astype(F32), axis=0, keepdims=True),
                      axis=1, keepdims=True).astype(jnp.int32)
    one = jnp.ones_like(tb)
    grp = jnp.where(tb >= 3 * PAIRS_PER_GROUP, 3 * one,
                    jnp.where(tb >= 2 * PAIRS_PER_GROUP, 2 * one, jnp.where(tb >= PAIRS_PER_GROUP, one, 0 * one)))
    p = tb - grp * PAIRS_PER_GROUP
    klo = jnp.where(p >= 5, 2 * one, jnp.where(p >= 3, one, 0 * one))
    khi = p - jnp.where(klo == 0, 0, jnp.where(klo == 1, 3, 5)) + klo + 1
    plan_ref[...] = jnp.zeros_like(plan_ref)
    plan_ref[PLAN_E_LO:PLAN_E_LO + 1, :] = grp * EXPERTS_PER_GROUP + klo
    plan_ref[PLAN_E_HI:PLAN_E_HI + 1, :] = grp * EXPERTS_PER_GROUP + khi
    plan_ref[PLAN_N_TILES:PLAN_N_TILES + 1, :] = jnp.broadcast_to(n_tiles, (1, LANES))


def _moe_plan(bucket):
    t = bucket.shape[1]
    return pl.pallas_call(
        functools.partial(_moe_plan_kernel, blk=512),
        out_shape=[jax.ShapeDtypeStruct((1, t), jnp.int32),
                   jax.ShapeDtypeStruct((PLAN_ROWS, LANES), jnp.int32)],
        compiler_params=pltpu.CompilerParams(vmem_limit_bytes=VMEM_LIMIT),
        name="moe_plan",
    )(bucket)


def _max_tiles(t):
    return (t + N_BUCKETS * (MOE_TILE - 1)) // MOE_TILE


def _row_copy(src, src_row, dst, dst_row, sem):
    return pltpu.make_async_copy(src.at[pl.ds(src_row, 1)], dst.at[pl.ds(dst_row, 1)], sem)


def _moe_scatter_kernel(pos_ref, x0_ref, x1_ref, tail_ref, buf_ref, out_ref, aug_ref, sem, *, n0):
    del buf_ref
    i = pl.program_id(0)
    tm, d = x0_ref.shape
    aug_ref[:, :d] = jnp.where(i < n0, x0_ref[...], x1_ref[...])
    aug_ref[:, d:] = tail_ref[...]

    for r in range(tm):
        _row_copy(aug_ref, r, out_ref, pos_ref[0, r], sem).start()
    pltpu.make_async_copy(aug_ref, out_ref.at[pl.ds(0, tm)], sem).wait()


def _moe_scatter(xs, streams, tail, pos3, tm):
    d = xs[0].shape[1]
    t = tail.shape[0]
    n0 = streams[0].rows // tm
    buf = jnp.zeros((_max_tiles(t) * MOE_TILE, d + TAIL), F32)
    return pl.pallas_call(
        functools.partial(_moe_scatter_kernel, n0=n0),
        grid=(t // tm,),
        in_specs=[pl.BlockSpec((None, 1, tm), lambda i: (i, 0, 0), memory_space=pltpu.SMEM),
                  *_two_stream_specs(streams, tm, d),
                  pl.BlockSpec((tm, TAIL), lambda i: (i, 0)),
                  pl.BlockSpec(memory_space=pl.ANY)],
        out_specs=pl.BlockSpec(memory_space=pl.ANY),
        out_shape=jax.ShapeDtypeStruct(buf.shape, F32),
        input_output_aliases={4: 0},
        scratch_shapes=[pltpu.VMEM((tm, d + TAIL), F32), pltpu.SemaphoreType.DMA(())],
        compiler_params=_params("arbitrary"),
        name="moe_scatter",
    )(pos3, *xs, tail, buf)


def _moe_expert_kernel(plan_ref, xa_ref, g_ref, mt_ref, wg0_ref, wu0_ref, wd0_ref, wg1_ref, wu1_ref, wd1_ref,
                       o_ref, wgu_ref, wd_ref):
    i = pl.program_id(0)
    n_tiles = plan_ref[PLAN_N_TILES * LANES]
    d = o_ref.shape[1]
    f = wg0_ref.shape[1]

    cur = jnp.minimum(i, n_tiles - 1)
    prev = jnp.minimum(jnp.maximum(i - 1, 0), n_tiles - 1)
    changed = ((i == 0) | (plan_ref[PLAN_E_LO * LANES + cur] != plan_ref[PLAN_E_LO * LANES + prev])
               | (plan_ref[PLAN_E_HI * LANES + cur] != plan_ref[PLAN_E_HI * LANES + prev]))

    @pl.when(changed)
    def _():
        for k, (wg_ref, wu_ref, wdn_ref) in enumerate(((wg0_ref, wu0_ref, wd0_ref), (wg1_ref, wu1_ref, wd1_ref))):
            wgu_ref[k, :, :f] = wg_ref[...].astype(BF16)
            wgu_ref[k, :, f:] = wu_ref[...].astype(BF16)
            wd_ref[k] = wdn_ref[...].astype(BF16)

    @pl.when(i < n_tiles)
    def _():
        x = xa_ref[:, :d]
        gate_lo = xa_ref[:, d:d + 1]
        gate_hi = xa_ref[:, d + 1:d + 2]
        mrow = xa_ref[:, d + 2:d + 3]

        def pick(kind):
            base = 3 * kind
            return jnp.where(mrow < 0.5, mt_ref[base:base + 1, :],
                             jnp.where(mrow < 1.5, mt_ref[base + 1:base + 2, :], mt_ref[base + 2:base + 3, :]))

        h = _adaln(x, g_ref[...], pick(0), pick(1)).astype(BF16)
        out = None
        for k, gate in enumerate((gate_lo, gate_hi)):
            gu = jnp.dot(h, wgu_ref[k], preferred_element_type=F32)
            act = (_silu(gu[:, :f]) * gu[:, f:] * gate).astype(BF16)
            y = jnp.dot(act, wd_ref[k], preferred_element_type=F32)
            out = y if out is None else out + y
        o_ref[...] = x + pick(2) * out

    @pl.when(i >= n_tiles)
    def _():
        o_ref[...] = jnp.zeros_like(o_ref)


def _moe_experts(xa, plan, g, mtab, layer, wg, wu, wd):
    rows, da = xa.shape
    d = da - TAIL
    f = wg.shape[3]
    nt = rows // MOE_TILE

    def tile(i, plan_ref):
        return jnp.minimum(i, plan_ref[PLAN_N_TILES * LANES] - 1)

    def wspec(shape, row):
        return pl.BlockSpec((None, None, *shape), lambda i, p: (layer, p[row * LANES + tile(i, p)], 0, 0))

    grid_spec = pltpu.PrefetchScalarGridSpec(
        num_scalar_prefetch=1,
        grid=(nt,),
        in_specs=[pl.BlockSpec((MOE_TILE, da), lambda i, p: (tile(i, p), 0)),
                  pl.BlockSpec((1, d), lambda i, p: (0, 0)),
                  pl.BlockSpec(mtab.shape, lambda i, p: (0, 0)),
                  wspec((d, f), PLAN_E_LO), wspec((d, f), PLAN_E_LO), wspec((f, d), PLAN_E_LO),
                  wspec((d, f), PLAN_E_HI), wspec((d, f), PLAN_E_HI), wspec((f, d), PLAN_E_HI)],
        out_specs=pl.BlockSpec((MOE_TILE, d), lambda i, p: (i, 0)),
        scratch_shapes=[pltpu.VMEM((2, d, 2 * f), BF16), pltpu.VMEM((2, f, d), BF16)])
    return pl.pallas_call(
        _moe_expert_kernel,
        grid_spec=grid_spec,
        out_shape=jax.ShapeDtypeStruct((rows, d), F32),
        compiler_params=_params("arbitrary"),
        name="moe_experts",
    )(plan.reshape(-1), xa, g, mtab, wg, wu, wd, wg, wu, wd)


def _moe_gather_kernel(pos_ref, ys_ref, o_ref, sem):
    tm = o_ref.shape[0]

    for r in range(tm):
        _row_copy(ys_ref, pos_ref[0, r], o_ref, r, sem).start()
    pltpu.make_async_copy(ys_ref.at[pl.ds(0, tm)], o_ref, sem).wait()


def _moe_gather(ys, pos3, tile0, rows, tm):
    d = ys.shape[1]
    return pl.pallas_call(
        _moe_gather_kernel,
        grid=(rows // tm,),
        in_specs=[pl.BlockSpec((None, 1, tm), lambda i: (i + tile0, 0, 0), memory_space=pltpu.SMEM),
                  pl.BlockSpec(memory_space=pl.ANY)],
        out_specs=pl.BlockSpec((tm, d), lambda i: (i, 0)),
        out_shape=jax.ShapeDtypeStruct((rows, d), F32),
        scratch_shapes=[pltpu.SemaphoreType.DMA(())],
        compiler_params=_params("arbitrary"),
        name="moe_gather",
    )(pos3, ys)


def _moe(xs, streams, g, mod, layer, rw_t, rb_col, wg, wu, wd, tm=512):
    tail, bucket = _moe_route(xs, streams, g, mod, layer, rw_t, rb_col)
    pos, plan = _moe_plan(bucket)
    pos3 = pos.reshape(-1, 1, tm)
    xa = _moe_scatter(xs, streams, tail, pos3, tm)
    mtab = mod[layer, 3:6, :, 0, :].reshape(-1, xs[0].shape[1])
    ys = _moe_experts(xa, plan, g, mtab, layer, wg, wu, wd)
    out, tile0 = [], 0
    for st in streams:
        out.append(_moe_gather(ys, pos3, tile0, st.rows, tm))
        tile0 += st.rows // tm
    return out


def _rope_tables(t_len):
    rows = t_len // GRID_W
    row_ids = jnp.repeat(jnp.arange(rows), GRID_W).astype(F32)
    col_ids = jnp.tile(jnp.arange(GRID_W), rows).astype(F32)
    n_freq = HEAD_DIM // 4
    inv = 1.0 / (ROPE_THETA ** (jnp.arange(n_freq, dtype=F32) / n_freq))
    ang = jnp.concatenate([row_ids[:, None] * inv, col_ids[:, None] * inv], axis=-1)
    cos, sin = jnp.cos(ang), jnp.sin(ang)
    cs = jnp.tile(jnp.concatenate([cos, cos], axis=-1), (1, LANES // HEAD_DIM))
    sn = jnp.tile(jnp.concatenate([-sin, sin], axis=-1), (1, LANES // HEAD_DIM))
    return cs, sn


def _pad_lanes(v):
    return jnp.pad(v.reshape(1, -1), ((0, 0), (0, LANES - v.size)))


def kernel(x_prompt, x_sample, c, c_ctx, cache_k, cache_v, state_ssm, mod_w, mod_b, norm1_g, norm2_g,
           attn_w_qkv, attn_q_norm_g, attn_k_norm_g, attn_w_o, ssd_in_proj, ssd_conv_w, ssd_conv_b,
           ssd_dt_bias, ssd_a_log, ssd_d, ssd_norm_g, ssd_out_proj, router_w, router_b, expert_w_gate,
           expert_w_up, expert_w_down):
    n_p, l_p, d = x_prompt.shape
    n_s, l_s, _ = x_sample.shape
    depth = mod_w.shape[0]
    past = cache_k.shape[2]
    nkv = N_KV_HEADS * HEAD_DIM
    assert d == D_MODEL and 1 + n_s == 3
    streams = (_Stream(n_p, l_p, 0, 0), _Stream(n_s, l_s, 1, 1))
    xs = [x_prompt.reshape(n_p * l_p, d), x_sample.reshape(n_s * l_s, d)]

    cond8 = jnp.concatenate([c_ctx[None], c, jnp.zeros((8 - 1 - n_s, d), F32)], axis=0)
    mod = _modulation(cond8, mod_w, mod_b)
    mod = mod[:, :1 + n_s].reshape(depth, 1 + n_s, 6, 1, d).transpose(0, 2, 1, 3, 4)

    rope = _rope_tables(l_s)
    rw_t = jnp.pad(router_w.T, ((0, LANES - N_EXPERTS), (0, 0))).astype(BF16)
    rb_col = jnp.pad(router_b, (0, LANES - N_EXPERTS)).reshape(LANES, 1)

    new_k, new_v, new_ssm = [], [], []
    for layer in range(depth):
        i = layer // 2
        g1 = norm1_g[layer][None]
        if layer % 2 == 0:
            w_qkv = attn_w_qkv[i].astype(BF16)
            w_o = attn_w_o[i].astype(BF16)
            qg2 = jnp.tile(attn_q_norm_g[i], LANES // HEAD_DIM)[None]
            kg2 = jnp.tile(attn_k_norm_g[i], LANES // HEAD_DIM)[None]
            for si, st in enumerate(streams):
                qkv = _ada_mm(xs[si], g1, mod, layer, st, w_qkv)
                if si == 0:
                    q, k2, v2, kc, vc = _qk_prep(qkv, qg2, kg2, emit_cache=True)
                    o = _attention(q, k2, v2, st.n_seq, st.seq_len)
                    new_k.append(kc.reshape(n_p, l_p, N_KV_HEADS, HEAD_DIM))
                    new_v.append(vc.reshape(n_p, l_p, N_KV_HEADS, HEAD_DIM))
                else:
                    q, k2, v2 = _qk_prep(qkv, qg2, kg2, tables=rope)
                    kc2, vc2 = _kv_expand(cache_k[:, i].reshape(n_s * past, nkv),
                                          cache_v[:, i].reshape(n_s * past, nkv))
                    o = _attention(q, k2, v2, st.n_seq, st.seq_len, ctx=(kc2, vc2, past))
                xs[si] = _mm_res(o, w_o, xs[si], mod, layer, st)
        else:
            w_in = ssd_in_proj[i]
            nzx = D_INNER + CONV_DIM
            w_zx = w_in[:, :nzx].astype(BF16)
            w_dt = jnp.pad(w_in[:, nzx:], ((0, 0), (0, LANES - 2 * SSD_HEADS))).astype(BF16)
            w_out = ssd_out_proj[i].astype(BF16)
            dtb = _pad_lanes(ssd_dt_bias[i])
            alog = _pad_lanes(ssd_a_log[i])
            dskip = jnp.repeat(ssd_d[i], SSD_HEAD_DIM)[None]
            cw, cb = ssd_conv_w[i], ssd_conv_b[i][None]
            for si, st in enumerate(streams):
                z, xbc, dt = _in_proj_conv(xs[si], g1, mod, layer, st, w_zx, w_dt, cw, cb)
                init = None if si == 0 else state_ssm[:, i].reshape(n_s, 2, D_INNER, D_STATE)
                y_f, s_f = _ssd_scan(xbc, dt, dtb, alog, st, rev=False, init=init)
                yg, s_b = _ssd_scan(xbc, dt, dtb, alog, st, rev=True, init=init, final=(y_f, z, dskip))
                if si == 0:
                    new_ssm.append(jnp.stack([s_f, s_b], axis=1).reshape(
                        n_p, 2, SSD_HEADS, SSD_HEAD_DIM, D_STATE))
                xs[si] = _mm_res(yg, w_out, xs[si], mod, layer, st, norm_g=ssd_norm_g[i][None])
        xs = _moe(xs, streams, norm2_g[layer][None], mod, layer, rw_t, rb_col,
                  expert_w_gate, expert_w_up, expert_w_down)

    return (xs[0].reshape(n_p, l_p, d), xs[1].reshape(n_s, l_s, d),
            jnp.stack(new_k, axis=1), jnp.stack(new_v, axis=1), jnp.stack(new_ssm, axis=1))
```

```python
import functools
import math
from typing import NamedTuple

import jax
import jax.numpy as jnp
from jax import lax
from jax.experimental import pallas as pl
from jax.experimental.pallas import tpu as pltpu

F32 = jnp.float32
BF16 = jnp.bfloat16

D_MODEL = 1024
GRID_W = 64
N_HEADS = 16
N_KV_HEADS = 4
HEAD_DIM = 64
KV_REP = N_HEADS // N_KV_HEADS
ROPE_THETA = 10000.0
D_INNER = 2048
SSD_HEAD_DIM = 64
SSD_HEADS = 32
SSD_GROUPS = 4
SSD_REP = SSD_HEADS // SSD_GROUPS
D_STATE = 128
D_CONV = 5
CHUNK = 128
CONV_DIM = D_INNER + 2 * SSD_GROUPS * D_STATE
N_EXPERTS = 16
N_EXPERT_GROUPS = 4
EXPERTS_PER_GROUP = 4
D_FF_EXPERT = 256
EPS = 1e-6

LANES = 128
VMEM_LIMIT = 56 * 1024 * 1024


def _params(*sem):
    return pltpu.CompilerParams(dimension_semantics=sem, vmem_limit_bytes=VMEM_LIMIT)


def _silu(x):
    return x * jax.nn.sigmoid(x)


class _Stream(NamedTuple):
    n_seq: int
    seq_len: int
    mod_base: int
    mod_step: int

    @property
    def rows(self):
        return self.n_seq * self.seq_len


def _mod_kernel(c_ref, w_ref, b_ref, o_ref):
    s = _silu(c_ref[...])
    o_ref[...] = jnp.dot(s.astype(BF16), w_ref[...].astype(BF16),
                         preferred_element_type=F32) + b_ref[...]


def _modulation(cond8, mod_w, mod_b):
    depth, d, n = mod_w.shape
    tn = 1536
    return pl.pallas_call(
        _mod_kernel,
        grid=(depth, n // tn),
        in_specs=[pl.BlockSpec((8, d), lambda l, j: (0, 0)),
                  pl.BlockSpec((None, d, tn), lambda l, j: (l, 0, j)),
                  pl.BlockSpec((None, 1, tn), lambda l, j: (l, 0, j))],
        out_specs=pl.BlockSpec((None, 8, tn), lambda l, j: (l, 0, j)),
        out_shape=jax.ShapeDtypeStruct((depth, 8, n), F32),
        compiler_params=_params("parallel", "parallel"),
        name="modulation",
    )(cond8, mod_w, mod_b.reshape(depth, 1, n))


def _mod_spec(layer, chunk, tm, st):
    def imap(i, *_):
        return (layer, chunk, st.mod_base + st.mod_step * (i * tm // st.seq_len), 0, 0)
    return pl.BlockSpec((None, None, None, 1, D_MODEL), imap)


def _adaln(x, g, sh, sc):
    ms = jnp.mean(x * x, axis=-1, keepdims=True)
    y = x * lax.rsqrt(ms + EPS) * g
    return y * (1 + sc) + sh


def _mm_res_kernel(a_ref, w_ref, r_ref, gate_ref, o_ref):
    y = jnp.dot(a_ref[...], w_ref[...], preferred_element_type=F32)
    o_ref[...] = r_ref[...] + gate_ref[...] * y


def _norm_mm_res_kernel(a_ref, ng_ref, w_ref, r_ref, gate_ref, o_ref):
    a = a_ref[...]
    ms = jnp.mean(a * a, axis=-1, keepdims=True)
    an = (a * lax.rsqrt(ms + EPS) * ng_ref[...]).astype(BF16)
    y = jnp.dot(an, w_ref[...], preferred_element_type=F32)
    o_ref[...] = r_ref[...] + gate_ref[...] * y


def _mm_res(a, w, res, mod, layer, st, norm_g=None, tm=512):
    t, k = a.shape
    n = w.shape[1]
    a_spec = pl.BlockSpec((tm, k), lambda i: (i, 0))
    w_spec = pl.BlockSpec((k, n), lambda i: (0, 0))
    r_spec = pl.BlockSpec((tm, n), lambda i: (i, 0))
    gate_spec = _mod_spec(layer, 2, tm, st)
    if norm_g is None:
        kern, specs, args = _mm_res_kernel, [a_spec, w_spec, r_spec, gate_spec], (a, w, res, mod)
    else:
        kern = _norm_mm_res_kernel
        specs = [a_spec, pl.BlockSpec((1, k), lambda i: (0, 0)), w_spec, r_spec, gate_spec]
        args = (a, norm_g, w, res, mod)
    return pl.pallas_call(
        kern,
        grid=(t // tm,),
        in_specs=specs,
        out_specs=pl.BlockSpec((tm, n), lambda i: (i, 0)),
        out_shape=jax.ShapeDtypeStruct((t, n), F32),
        compiler_params=_params("parallel"),
        name="mm_res",
    )(*args)


def _lane_iota(rows):
    return lax.broadcasted_iota(jnp.int32, (rows, LANES), 1)


def _head_pair_norm(c, g2):
    lo = _lane_iota(c.shape[0]) < HEAD_DIM
    cc = c * c
    s_lo = jnp.sum(jnp.where(lo, cc, 0.0), axis=-1, keepdims=True)
    s_hi = jnp.sum(jnp.where(lo, 0.0, cc), axis=-1, keepdims=True)
    r = jnp.where(lo, lax.rsqrt(s_lo / HEAD_DIM + EPS), lax.rsqrt(s_hi / HEAD_DIM + EPS))
    return c * r * g2


def _rope_pair(y, cs, sn):
    first = (_lane_iota(y.shape[0]) % HEAD_DIM) < HEAD_DIM // 2
    partner = jnp.where(first, pltpu.roll(y, LANES - HEAD_DIM // 2, 1), pltpu.roll(y, HEAD_DIM // 2, 1))
    return y * cs + partner * sn


def _store_v_split(ref, j, c):
    lane = _lane_iota(c.shape[0])
    lo = lane < HEAD_DIM
    a_lo = jnp.where(lo, c, jnp.where(lane == HEAD_DIM, 1.0, 0.0))
    b_hi = jnp.where(lo, jnp.where(lane == 0, 1.0, 0.0), c)
    ref[2 * j, 0] = a_lo.astype(BF16)
    ref[2 * j, 1] = pltpu.roll(a_lo, HEAD_DIM, 1).astype(BF16)
    ref[2 * j + 1, 0] = pltpu.roll(b_hi, HEAD_DIM, 1).astype(BF16)
    ref[2 * j + 1, 1] = b_hi.astype(BF16)


def _store_kt_split(ref, j, c):
    t = c.T
    a, b = t[:HEAD_DIM], t[HEAD_DIM:]
    z = jnp.zeros_like(a)
    ref[2 * j, 0] = jnp.concatenate([a, z], axis=0).astype(BF16)
    ref[2 * j, 1] = jnp.concatenate([z, a], axis=0).astype(BF16)
    ref[2 * j + 1, 0] = jnp.concatenate([b, z], axis=0).astype(BF16)
    ref[2 * j + 1, 1] = jnp.concatenate([z, b], axis=0).astype(BF16)


def _qk_prep_kernel(*refs, rope, emit_cache):
    it = iter(refs)
    qkv_ref, qg_ref, kg_ref = next(it), next(it), next(it)
    cs_ref, sn_ref = (next(it), next(it)) if rope else (None, None)
    q_ref, k2_ref, v2_ref = next(it), next(it), next(it)
    kc_ref, vc_ref = (next(it), next(it)) if emit_cache else (None, None)

    nq = N_HEADS * HEAD_DIM
    nkv = N_KV_HEADS * HEAD_DIM
    scale = HEAD_DIM ** -0.5 * math.log2(math.e)
    for j in range(nq // LANES):
        y = _head_pair_norm(qkv_ref[:, j * LANES:(j + 1) * LANES], qg_ref[...])
        if rope:
            y = _rope_pair(y, cs_ref[...], sn_ref[...])
        q_ref[:, j * LANES:(j + 1) * LANES] = (y * scale).astype(BF16)
    for j in range(nkv // LANES):
        y = _head_pair_norm(qkv_ref[:, nq + j * LANES:nq + (j + 1) * LANES], kg_ref[...])
        if rope:
            y = _rope_pair(y, cs_ref[...], sn_ref[...])
        if emit_cache:
            kc_ref[:, j * LANES:(j + 1) * LANES] = y
        _store_kt_split(k2_ref, j, y)
        v = qkv_ref[:, nq + nkv + j * LANES:nq + nkv + (j + 1) * LANES]
        if emit_cache:
            vc_ref[:, j * LANES:(j + 1) * LANES] = v
        _store_v_split(v2_ref, j, v)


def _kv_shapes(rows):
    return (jax.ShapeDtypeStruct((N_KV_HEADS, 2, LANES, rows), BF16),
            jax.ShapeDtypeStruct((N_KV_HEADS, 2, rows, LANES), BF16))


def _kv_specs(tm):
    return (pl.BlockSpec((N_KV_HEADS, 2, LANES, tm), lambda i: (0, 0, 0, i)),
            pl.BlockSpec((N_KV_HEADS, 2, tm, LANES), lambda i: (0, 0, i, 0)))


def _qkv_kernel(x_ref, g_ref, sh_ref, sc_ref, w_ref, *refs, rope, emit_cache):
    qkv_ref = refs[-1]
    h = _adaln(x_ref[...], g_ref[...], sh_ref[...], sc_ref[...]).astype(BF16)
    qkv_ref[...] = jnp.dot(h, w_ref[...], preferred_element_type=F32)
    _qk_prep_kernel(qkv_ref, *refs[:-1], rope=rope, emit_cache=emit_cache)


def _qk_prep(x, g, mod, layer, st, w, qg2, kg2, tables=None, emit_cache=False, tm=512):
    rows, d = x.shape
    n = w.shape[1]
    nq = N_HEADS * HEAD_DIM
    nkv = N_KV_HEADS * HEAD_DIM
    rope = tables is not None
    in_specs = [pl.BlockSpec((tm, d), lambda i: (i, 0)),
                pl.BlockSpec((1, d), lambda i: (0, 0)),
                _mod_spec(layer, 0, tm, st),
                _mod_spec(layer, 1, tm, st),
                pl.BlockSpec((d, n), lambda i: (0, 0)),
                pl.BlockSpec((1, LANES), lambda i: (0, 0)),
                pl.BlockSpec((1, LANES), lambda i: (0, 0))]
    args = [x, g, mod, mod, w, qg2, kg2]
    if rope:
        nt = tables[0].shape[0] // tm
        in_specs += [pl.BlockSpec((tm, LANES), lambda i: (i % nt, 0))] * 2
        args += list(tables)
    out_shape = [jax.ShapeDtypeStruct((rows, nq), BF16), *_kv_shapes(rows)]
    out_specs = [pl.BlockSpec((tm, nq), lambda i: (i, 0)), *_kv_specs(tm)]
    if emit_cache:
        out_shape += [jax.ShapeDtypeStruct((rows, nkv), F32)] * 2
        out_specs += [pl.BlockSpec((tm, nkv), lambda i: (i, 0))] * 2
    return pl.pallas_call(
        functools.partial(_qkv_kernel, rope=rope, emit_cache=emit_cache),
        grid=(rows // tm,),
        in_specs=in_specs,
        out_specs=out_specs,
        out_shape=out_shape,
        scratch_shapes=[pltpu.VMEM((tm, n), F32)],
        compiler_params=_params("parallel"),
        name="qkv_prep",
    )(*args)


def _kv_expand_kernel(k_ref, v_ref, k2_ref, v2_ref):
    for j in range(N_KV_HEADS * HEAD_DIM // LANES):
        _store_kt_split(k2_ref, j, k_ref[:, j * LANES:(j + 1) * LANES])
        _store_v_split(v2_ref, j, v_ref[:, j * LANES:(j + 1) * LANES])


def _kv_expand(k, v, tm=512):
    rows, nkv = k.shape
    return pl.pallas_call(
        _kv_expand_kernel,
        grid=(rows // tm,),
        in_specs=[pl.BlockSpec((tm, nkv), lambda i: (i, 0))] * 2,
        out_specs=list(_kv_specs(tm)),
        out_shape=list(_kv_shapes(rows)),
        compiler_params=_params("parallel"),
        name="kv_expand",
    )(k, v)


def _attn_kernel(*refs, tk, n_tiles, has_ctx):
    if has_ctx:
        q_ref, k_ref, v_ref, kc_ref, vc_ref, o_ref = refs
    else:
        q_ref, k_ref, v_ref, o_ref = refs
        kc_ref = vc_ref = None
    tq = q_ref.shape[0]
    lo = _lane_iota(tq) < HEAD_DIM
    tiles = [(k_ref, v_ref, t) for t in range(n_tiles)]
    if has_ctx:
        tiles += [(kc_ref, vc_ref, t) for t in range(vc_ref.shape[2] // tk)]
    for gi in range(k_ref.shape[0]):
        for pair in range(KV_REP // 2):
            c0 = (gi * (KV_REP // 2) + pair) * LANES
            qp = q_ref[:, c0:c0 + LANES]
            m = [jnp.full((tq, 1), -jnp.inf, F32)] * 2
            acc = [jnp.zeros((tq, LANES), F32)] * 2
            for kr, vr, t in tiles:
                for half in range(2):
                    s = jnp.dot(qp, kr[gi, half, :, t * tk:(t + 1) * tk], preferred_element_type=F32)
                    m_new = jnp.maximum(m[half], jnp.max(s, axis=-1, keepdims=True))
                    p = jnp.exp2(s - m_new).astype(BF16)
                    acc[half] = jnp.exp2(m[half] - m_new) * acc[half] + jnp.dot(
                        p, vr[gi, half, t * tk:(t + 1) * tk, :], preferred_element_type=F32)
                    m[half] = m_new
            out_lo = acc[0] / acc[0][:, HEAD_DIM:HEAD_DIM + 1]
            out_hi = acc[1] / acc[1][:, 0:1]
            o_ref[:, c0:c0 + LANES] = jnp.where(lo, out_lo, out_hi).astype(BF16)


def _attention(q, k2, v2, n_batch, t_len, ctx=None, tq=512, tk=512, groups=1):
    tk = min(tk, t_len)
    tq = min(tq, t_len)
    nq = t_len // tq
    gw = groups * KV_REP * HEAD_DIM
    q_spec = pl.BlockSpec((tq, gw), lambda b, g, i: (b * nq + i, g))
    in_specs = [q_spec,
                pl.BlockSpec((groups, 2, LANES, t_len), lambda b, g, i: (g, 0, 0, b)),
                pl.BlockSpec((groups, 2, t_len, LANES), lambda b, g, i: (g, 0, b, 0))]
    args = [q, k2, v2]
    if ctx is not None:
        kc2, vc2, ctx_len = ctx
        in_specs += [pl.BlockSpec((groups, 2, LANES, ctx_len), lambda b, g, i: (g, 0, 0, b)),
                     pl.BlockSpec((groups, 2, ctx_len, LANES), lambda b, g, i: (g, 0, b, 0))]
        args += [kc2, vc2]
    return pl.pallas_call(
        functools.partial(_attn_kernel, tk=tk, n_tiles=t_len // tk, has_ctx=ctx is not None),
        grid=(n_batch, N_KV_HEADS // groups, nq),
        in_specs=in_specs,
        out_specs=q_spec,
        out_shape=jax.ShapeDtypeStruct(q.shape, BF16),
        compiler_params=_params("parallel", "parallel", "arbitrary"),
        name="attention",
    )(*args)


HALO = 16


def _in_proj_kernel(x_ref, xa_ref, xb_ref, g_ref, sh_ref, sc_ref, w_ref, wdt_ref, cw_ref, cb_ref,
                    z_ref, xbc_ref, dt_ref, h_ref, u_ref, *, n_z_tiles, seq_len):
    i, j = pl.program_id(0), pl.program_id(1)
    tm = x_ref.shape[0]

    @pl.when(j == 0)
    def _():
        def norm(ref):
            return _adaln(ref[...], g_ref[...], sh_ref[...], sc_ref[...]).astype(BF16)
        h_ref[0:HALO] = norm(xa_ref)
        h = norm(x_ref)
        h_ref[HALO:HALO + tm] = h
        h_ref[HALO + tm:] = norm(xb_ref)
        dt_ref[...] = jnp.dot(h, wdt_ref[...], preferred_element_type=F32)

    @pl.when(j < n_z_tiles)
    def _():
        z_ref[...] = jnp.dot(h_ref[HALO:HALO + tm], w_ref[...], preferred_element_type=F32)

    @pl.when(j >= n_z_tiles)
    def _():
        u = jnp.dot(h_ref[...], w_ref[...], preferred_element_type=F32)
        keep_a = ((i * tm) % seq_len != 0).astype(F32)
        keep_b = (((i + 1) * tm) % seq_len != 0).astype(F32)
        u_ref[0:HALO] = u[0:HALO] * keep_a
        u_ref[HALO:HALO + tm] = u[HALO:HALO + tm]
        u_ref[HALO + tm:] = u[HALO + tm:] * keep_b
        acc = cb_ref[...]
        for k in range(D_CONV):
            d = k - D_CONV // 2
            acc = acc + u_ref[HALO + d:HALO + d + tm, :] * cw_ref[k:k + 1, :]
        xbc_ref[...] = _silu(acc)


def _in_proj_conv(x, g, mod, layer, st, w_zx, w_dt, conv_w, conv_b, tm=512, tn=1024):
    t, d = x.shape
    tm = min(tm, st.seq_len)
    assert st.seq_len % tm == 0 and D_INNER % tn == 0 and CONV_DIM % tn == 0
    nzt, nct = D_INNER // tn, CONV_DIM // tn
    hb = tm // HALO
    last_halo = t // HALO - 1
    return pl.pallas_call(
        functools.partial(_in_proj_kernel, n_z_tiles=nzt, seq_len=st.seq_len),
        grid=(t // tm, nzt + nct),
        in_specs=[pl.BlockSpec((tm, d), lambda i, j: (i, 0)),
                  pl.BlockSpec((HALO, d), lambda i, j: (jnp.maximum(i * hb - 1, 0), 0)),
                  pl.BlockSpec((HALO, d), lambda i, j: (jnp.minimum((i + 1) * hb, last_halo), 0)),
                  pl.BlockSpec((1, d), lambda i, j: (0, 0)),
                  _mod_spec(layer, 0, tm, st),
                  _mod_spec(layer, 1, tm, st),
                  pl.BlockSpec((d, tn), lambda i, j: (0, j)),
                  pl.BlockSpec((d, LANES), lambda i, j: (0, 0)),
                  pl.BlockSpec((D_CONV, tn), lambda i, j: (0, jnp.maximum(j - nzt, 0))),
                  pl.BlockSpec((1, tn), lambda i, j: (0, jnp.maximum(j - nzt, 0)))],
        out_specs=[pl.BlockSpec((tm, tn), lambda i, j: (i, jnp.minimum(j, nzt - 1))),
                   pl.BlockSpec((tm, tn), lambda i, j: (i, jnp.maximum(j - nzt, 0))),
                   pl.BlockSpec((tm, LANES), lambda i, j: (i, 0))],
        out_shape=[jax.ShapeDtypeStruct((t, D_INNER), F32),
                   jax.ShapeDtypeStruct((t, CONV_DIM), F32),
                   jax.ShapeDtypeStruct((t, LANES), F32)],
        scratch_shapes=[pltpu.VMEM((tm + 2 * HALO, d), BF16), pltpu.VMEM((tm + 2 * HALO, tn), F32)],
        compiler_params=_params("parallel", "arbitrary"),
        name="in_proj_conv",
    )(x, x, x, g, mod, mod, w_zx, w_dt, conv_w, conv_b)


def _cumsum_rows(v, reverse):
    n = v.shape[0]
    row = lax.broadcasted_iota(jnp.int32, v.shape, 0)
    k = 1
    while k < n:
        if reverse:
            v = v + jnp.where(row < n - k, pltpu.roll(v, n - k, 0), 0.0)
        else:
            v = v + jnp.where(row >= k, pltpu.roll(v, k, 0), 0.0)
        k *= 2
    return v


def _ssd_kernel(*refs, rev, final_pass, has_init, cps):
    it = iter(refs)
    x_ref, b_ref, c_ref, dt_ref, dtb_ref, alog_ref = [next(it) for _ in range(6)]
    init_ref = next(it) if has_init else None
    if final_pass:
        y0_ref, z_ref, dskip_ref = next(it), next(it), next(it)
    y_ref, fin_ref, st_ref = next(it), next(it), next(it)

    ci = pl.program_id(1)
    n_blk = D_INNER // LANES

    @pl.when(ci == 0)
    def _():
        if has_init:
            for k in range(n_blk):
                st_ref[:, k * LANES:(k + 1) * LANES] = init_ref[k * LANES:(k + 1) * LANES, :].T
        else:
            st_ref[...] = jnp.zeros_like(st_ref)

    dirn = 1 if rev else 0
    neg_a = -jnp.exp(alog_ref[...])
    row = lax.broadcasted_iota(jnp.int32, (CHUNK, CHUNK), 0)
    col = lax.broadcasted_iota(jnp.int32, (CHUNK, CHUNK), 1)
    causal = (col >= row) if rev else (col <= row)
    lo = _lane_iota(CHUNK) < SSD_HEAD_DIM
    last = 0 if rev else CHUNK - 1

    for cc in (reversed(range(cps)) if rev else range(cps)):
        rows = slice(cc * CHUNK, (cc + 1) * CHUNK)
        dtv = jax.nn.softplus(dt_ref[rows, :] + dtb_ref[...])
        acum = _cumsum_rows(dtv * neg_a, rev) * math.log2(math.e)
        acum_t = acum.T
        dtv_t = dtv.T
        w_t = dtv_t * jnp.exp2(acum_t[:, last:last + 1] - acum_t)
        e_tot = jnp.exp2(acum[last:last + 1, :])
        src_t = acum_t - jnp.log2(dtv_t)
        for g in range(SSD_GROUPS):
            bg = b_ref[rows, g * D_STATE:(g + 1) * D_STATE]
            cg = c_ref[rows, g * D_STATE:(g + 1) * D_STATE]
            cb = lax.dot_general(cg.astype(BF16), bg.astype(BF16), (((1,), (1,)), ((), ())),
                                 preferred_element_type=F32)
            bg_t = bg.T
            for pp in range(SSD_REP // 2):
                h0 = g * SSD_REP + 2 * pp
                i0 = dirn * SSD_HEADS + h0
                cols = slice(h0 * SSD_HEAD_DIM, (h0 + 2) * SSD_HEAD_DIM)
                xp = x_ref[rows, cols]
                st_pair = st_ref[:, cols]
                halves = ((jnp.where(lo, xp, 0.0).astype(BF16), jnp.where(lo, st_pair, 0.0).astype(BF16)),
                          (jnp.where(lo, 0.0, xp).astype(BF16), jnp.where(lo, 0.0, st_pair).astype(BF16)))
                y = new = None
                for hh, (xh, sth) in enumerate(halves):
                    i = i0 + hh
                    a_col = jnp.broadcast_to(acum[:, i:i + 1], (CHUNK, CHUNK))
                    lmat = jnp.exp2(jnp.where(causal, a_col - src_t[i:i + 1, :], -jnp.inf))
                    m = (cb * lmat).astype(BF16)
                    ce = (cg * jnp.exp2(a_col)).astype(BF16)
                    yh = jnp.dot(jnp.concatenate([m, ce], axis=1), jnp.concatenate([xh, sth], axis=0),
                                 preferred_element_type=F32)
                    nh = jnp.dot((bg_t * w_t[i:i + 1, :]).astype(BF16), xh, preferred_element_type=F32)
                    y = yh if y is None else y + yh
                    new = nh if new is None else new + nh
                dec = jnp.where(lo[:1], e_tot[:, i0:i0 + 1], e_tot[:, i0 + 1:i0 + 2])
                st_ref[:, cols] = st_pair * dec + new
                if final_pass:
                    y = (y0_ref[rows, cols] + y + xp * dskip_ref[:, cols]) * _silu(z_ref[rows, cols])
                y_ref[rows, cols] = y

    @pl.when(ci == pl.num_programs(1) - 1)
    def _():
        for k in range(n_blk):
            fin_ref[k * LANES:(k + 1) * LANES, :] = st_ref[:, k * LANES:(k + 1) * LANES].T


def _ssd_scan(xbc, dt, dtb, alog, st, rev, init=None, final=None, cps=4):
    cps = min(cps, st.seq_len // CHUNK)
    blk = cps * CHUNK
    nb = st.seq_len // blk
    dirn = 1 if rev else 0

    def tok(s, c):
        return (s * nb + (nb - 1 - c if rev else c), 0)

    bc_w = SSD_GROUPS * D_STATE
    b_col = D_INNER // bc_w
    tok_spec = pl.BlockSpec((blk, D_INNER), tok)
    in_specs = [tok_spec,
                pl.BlockSpec((blk, bc_w), lambda s, c: (tok(s, c)[0], b_col)),
                pl.BlockSpec((blk, bc_w), lambda s, c: (tok(s, c)[0], b_col + 1)),
                pl.BlockSpec((blk, LANES), tok),
                pl.BlockSpec((1, LANES), lambda s, c: (0, 0)),
                pl.BlockSpec((1, LANES), lambda s, c: (0, 0))]
    args = [xbc, xbc, xbc, dt, dtb, alog]
    if init is not None:
        in_specs.append(pl.BlockSpec((None, None, D_INNER, D_STATE), lambda s, c: (s, dirn, 0, 0)))
        args.append(init)
    if final is not None:
        y0, zx, dskip = final
        in_specs += [tok_spec, tok_spec, pl.BlockSpec((1, D_INNER), lambda s, c: (0, 0))]
        args += [y0, zx, dskip]
    return pl.pallas_call(
        functools.partial(_ssd_kernel, rev=rev, final_pass=final is not None,
                          has_init=init is not None, cps=cps),
        grid=(st.n_seq, nb),
        in_specs=in_specs,
        out_specs=[tok_spec, pl.BlockSpec((None, D_INNER, D_STATE), lambda s, c: (s, 0, 0))],
        out_shape=[jax.ShapeDtypeStruct((st.rows, D_INNER), F32),
                   jax.ShapeDtypeStruct((st.n_seq, D_INNER, D_STATE), F32)],
        scratch_shapes=[pltpu.VMEM((D_STATE, D_INNER), F32)],
        compiler_params=_params("parallel", "arbitrary"),
        name="ssd_scan",
    )(*args)


MOE_TILE = 256
PAIRS_PER_GROUP = EXPERTS_PER_GROUP * (EXPERTS_PER_GROUP - 1) // 2
N_BUCKETS = N_EXPERT_GROUPS * PAIRS_PER_GROUP
BUCKET_ROWS = 32
TAIL = LANES


def _route_picks(biased_t, scores_t):
    ng, ne = N_EXPERT_GROUPS, EXPERTS_PER_GROUP
    b = [[biased_t[j * ne + k:j * ne + k + 1, :] for k in range(ne)] for j in range(ng)]
    s = [[scores_t[j * ne + k:j * ne + k + 1, :] for k in range(ne)] for j in range(ng)]
    gs = []
    for j in range(ng):
        best = None
        for k1 in range(ne):
            for k2 in range(k1 + 1, ne):
                ps = b[j][k1] + b[j][k2]
                best = ps if best is None else jnp.maximum(best, ps)
        gs.append(best)
    sel = jnp.zeros_like(gs[0], jnp.int32)
    best = gs[0]
    for j in range(1, ng):
        gt = gs[j] > best
        best = jnp.where(gt, gs[j], best)
        sel = jnp.where(gt, j, sel)
    in_grp = [sel == j for j in range(ng)]

    def pick(vals):
        out = []
        for k in range(ne):
            v = vals[ng - 1][k]
            for j in range(ng - 2, -1, -1):
                v = jnp.where(in_grp[j], vals[j][k], v)
            out.append(v)
        return out

    yb, ys = pick(b), pick(s)

    def argmax_first(vals):
        bv, bi = vals[0], jnp.zeros_like(sel)
        for k in range(1, ne):
            gt = vals[k] > bv
            bv = jnp.where(gt, vals[k], bv)
            bi = jnp.where(gt, k, bi)
        return bi

    i1 = argmax_first(yb)
    i2 = argmax_first([jnp.where(i1 == k, -jnp.inf, yb[k]) for k in range(ne)])

    def take(vals, idx):
        v = vals[ne - 1]
        for k in range(ne - 2, -1, -1):
            v = jnp.where(idx == k, vals[k], v)
        return v

    w1, w2 = take(ys, i1), take(ys, i2)
    tot = w1 + w2
    return sel, i1, i2, w1 / tot, w2 / tot


def _two_stream_specs(streams, tm, width):
    n0 = streams[0].rows // tm
    return (pl.BlockSpec((tm, width), lambda i, *_: (jnp.minimum(i, n0 - 1), 0)),
            pl.BlockSpec((tm, width), lambda i, *_: (jnp.maximum(i - n0, 0), 0)))


def _moe_route_kernel(x0_ref, x1_ref, g_ref, sh0_ref, sc0_ref, sh1_ref, sc1_ref, rw_ref, rb_ref,
                      tail_ref, bucket_ref, tt_ref, *, n0, mod_of_tile):
    i = pl.program_id(0)
    first = i < n0
    x = jnp.where(first, x0_ref[...], x1_ref[...])
    sh = jnp.where(first, sh0_ref[...], sh1_ref[...])
    sc = jnp.where(first, sc0_ref[...], sc1_ref[...])
    h = _adaln(x, g_ref[...], sh, sc).astype(BF16)
    logits_t = lax.dot_general(rw_ref[...], h, (((1,), (1,)), ((), ())),
                               preferred_element_type=F32)
    scores_t = jax.nn.sigmoid(logits_t)
    sel, i1, i2, g1, g2 = _route_picks(scores_t + rb_ref[...], scores_t)
    first_lo = i1 < i2
    klo = jnp.where(first_lo, i1, i2)
    khi = jnp.where(first_lo, i2, i1)
    pair = jnp.where(klo == 0, 0, jnp.where(klo == 1, 3, 5)) + (khi - klo - 1)
    bucket_ref[...] = sel * PAIRS_PER_GROUP + pair
    tt_ref[...] = jnp.zeros_like(tt_ref)
    tt_ref[0:1, :] = jnp.where(first_lo, g1, g2)
    tt_ref[1:2, :] = jnp.where(first_lo, g2, g1)
    tt_ref[2:3, :] = jnp.full(g1.shape, 1.0, F32) * mod_of_tile(i).astype(F32)
    tail_ref[...] = tt_ref[...].T


def _moe_route(xs, streams, g, mod, layer, rw_t, rb_col, tm=1024):
    d = xs[0].shape[1]
    t = sum(st.rows for st in streams)
    n0 = streams[0].rows // tm
    s0, s1 = streams

    def mod_of_tile(i):
        row0 = s0.mod_base + s0.mod_step * (i * tm // s0.seq_len)
        row1 = s1.mod_base + s1.mod_step * ((i - n0) * tm // s1.seq_len)
        return jnp.where(i < n0, row0, row1)

    def mspec(chunk, st, shift):
        def imap(i):
            return (layer, chunk, st.mod_base + st.mod_step * (jnp.maximum(i - shift, 0) * tm // st.seq_len), 0, 0)
        return pl.BlockSpec((None, None, None, 1, d), imap)

    x_specs = _two_stream_specs(streams, tm, d)
    return pl.pallas_call(
        functools.partial(_moe_route_kernel, n0=n0, mod_of_tile=mod_of_tile),
        grid=(t // tm,),
        in_specs=[*x_specs,
                  pl.BlockSpec((1, d), lambda i: (0, 0)),
                  mspec(3, s0, 0), mspec(4, s0, 0), mspec(3, s1, n0), mspec(4, s1, n0),
                  pl.BlockSpec((LANES, d), lambda i: (0, 0)),
                  pl.BlockSpec((LANES, 1), lambda i: (0, 0))],
        out_specs=[pl.BlockSpec((tm, TAIL), lambda i: (i, 0)),
                   pl.BlockSpec((1, tm), lambda i: (0, i))],
        out_shape=[jax.ShapeDtypeStruct((t, TAIL), F32), jax.ShapeDtypeStruct((1, t), jnp.int32)],
        scratch_shapes=[pltpu.VMEM((LANES, tm), F32)],
        compiler_params=_params("parallel"),
        name="moe_route",
    )(*xs, g, mod, mod, mod, mod, rw_t, rb_col)


PLAN_E_LO, PLAN_E_HI, PLAN_N_TILES, PLAN_ROWS = 0, 1, 2, 8


def _moe_plan_kernel(bucket_ref, pos_ref, plan_ref, *, blk):
    n_tok = bucket_ref.shape[1]
    bid = lax.broadcasted_iota(jnp.int32, (BUCKET_ROWS, blk), 0)
    r = lax.broadcasted_iota(jnp.int32, (blk, blk), 0)
    c = lax.broadcasted_iota(jnp.int32, (blk, blk), 1)
    upper = (r <= c).astype(BF16)

    def onehot(k):
        return (bucket_ref[:, k * blk:(k + 1) * blk] == bid).astype(F32)

    counts = jnp.zeros((BUCKET_ROWS, 1), F32)
    for k in range(n_tok // blk):
        counts = counts + jnp.sum(onehot(k), axis=1, keepdims=True)
    padded = jnp.floor((counts + (MOE_TILE - 1)) * (1.0 / MOE_TILE)) * MOE_TILE
    padded_b = jnp.broadcast_to(padded, (BUCKET_ROWS, LANES))
    starts_b = _cumsum_rows(padded_b, False) - padded_b
    starts = starts_b[:, 0:1]

    carry = jnp.zeros((BUCKET_ROWS, 1), F32)
    for k in range(n_tok // blk):
        oh = onehot(k)
        incl = jnp.dot(oh.astype(BF16), upper, preferred_element_type=F32)
        slot = jnp.sum(oh * (starts + carry + incl - 1.0), axis=0, keepdims=True)
        pos_ref[:, k * blk:(k + 1) * blk] = slot.astype(jnp.int32)
        carry = carry + incl[:, blk - 1:blk]

    tile0 = (lax.broadcasted_iota(jnp.int32, (BUCKET_ROWS, LANES), 1) * MOE_TILE).astype(F32)
    inside = (tile0 >= starts_b) & (tile0 < starts_b + padded_b)
    brow = lax.broadcasted_iota(jnp.int32, (BUCKET_ROWS, LANES), 0).astype(F32)
    tb = jnp.sum(jnp.where(inside, brow, 0.0), axis=0, keepdims=True).astype(jnp.int32)
    n_tiles = jnp.sum(jnp.sum(inside.astype(F32), axis=0, keepdims=True),
                      axis=1, keepdims=True).astype(jnp.int32)
    one = jnp.ones_like(tb)
    grp = jnp.where(tb >= 3 * PAIRS_PER_GROUP, 3 * one,
                    jnp.where(tb >= 2 * PAIRS_PER_GROUP, 2 * one, jnp.where(tb >= PAIRS_PER_GROUP, one, 0 * one)))
    p = tb - grp * PAIRS_PER_GROUP
    klo = jnp.where(p >= 5, 2 * one, jnp.where(p >= 3, one, 0 * one))
    khi = p - jnp.where(klo == 0, 0, jnp.where(klo == 1, 3, 5)) + klo + 1
    plan_ref[...] = jnp.zeros_like(plan_ref)
    plan_ref[PLAN_E_LO:PLAN_E_LO + 1, :] = grp * EXPERTS_PER_GROUP + klo
    plan_ref[PLAN_E_HI:PLAN_E_HI + 1, :] = grp * EXPERTS_PER_GROUP + khi
    plan_ref[PLAN_N_TILES:PLAN_N_TILES + 1, :] = jnp.broadcast_to(n_tiles, (1, LANES))


def _moe_plan(bucket):
    t = bucket.shape[1]
    return pl.pallas_call(
        functools.partial(_moe_plan_kernel, blk=512),
        out_shape=[jax.ShapeDtypeStruct((1, t), jnp.int32),
                   jax.ShapeDtypeStruct((PLAN_ROWS, LANES), jnp.int32)],
        compiler_params=pltpu.CompilerParams(vmem_limit_bytes=VMEM_LIMIT),
        name="moe_plan",
    )(bucket)


def _max_tiles(t):
    return (t + N_BUCKETS * (MOE_TILE - 1)) // MOE_TILE


def _row_copy(src, src_row, dst, dst_row, sem):
    return pltpu.make_async_copy(src.at[pl.ds(src_row, 1)], dst.at[pl.ds(dst_row, 1)], sem)


def _moe_scatter_kernel(pos_ref, x0_ref, x1_ref, tail_ref, buf_ref, out_ref, aug_ref, sem, *, n0):
    del buf_ref
    i = pl.program_id(0)
    tm, d = x0_ref.shape
    aug_ref[:, :d] = jnp.where(i < n0, x0_ref[...], x1_ref[...])
    aug_ref[:, d:] = tail_ref[...]

    for r in range(tm):
        _row_copy(aug_ref, r, out_ref, pos_ref[0, r], sem).start()
    pltpu.make_async_copy(aug_ref, out_ref.at[pl.ds(0, tm)], sem).wait()


def _sorted_buffer(t, d):
    return jnp.zeros((_max_tiles(t) * MOE_TILE, d + TAIL), F32)


def _moe_scatter(xs, streams, tail, pos3, tm, buf):
    d = xs[0].shape[1]
    t = tail.shape[0]
    n0 = streams[0].rows // tm
    return pl.pallas_call(
        functools.partial(_moe_scatter_kernel, n0=n0),
        grid=(t // tm,),
        in_specs=[pl.BlockSpec((None, 1, tm), lambda i: (i, 0, 0), memory_space=pltpu.SMEM),
                  *_two_stream_specs(streams, tm, d),
                  pl.BlockSpec((tm, TAIL), lambda i: (i, 0)),
                  pl.BlockSpec(memory_space=pl.ANY)],
        out_specs=pl.BlockSpec(memory_space=pl.ANY),
        out_shape=jax.ShapeDtypeStruct(buf.shape, F32),
        input_output_aliases={4: 0},
        scratch_shapes=[pltpu.VMEM((tm, d + TAIL), F32), pltpu.SemaphoreType.DMA(())],
        compiler_params=_params("arbitrary"),
        name="moe_scatter",
    )(pos3, *xs, tail, buf)


def _moe_expert_kernel(plan_ref, xa_ref, g_ref, mt_ref, wg0_ref, wu0_ref, wd0_ref, wg1_ref, wu1_ref, wd1_ref,
                       o_ref, wgu_ref, wd_ref):
    i = pl.program_id(0)
    n_tiles = plan_ref[PLAN_N_TILES * LANES]
    d = o_ref.shape[1]
    f = wg0_ref.shape[1]

    cur = jnp.minimum(i, n_tiles - 1)
    prev = jnp.minimum(jnp.maximum(i - 1, 0), n_tiles - 1)
    changed = ((i == 0) | (plan_ref[PLAN_E_LO * LANES + cur] != plan_ref[PLAN_E_LO * LANES + prev])
               | (plan_ref[PLAN_E_HI * LANES + cur] != plan_ref[PLAN_E_HI * LANES + prev]))

    @pl.when(changed)
    def _():
        for k, (wg_ref, wu_ref, wdn_ref) in enumerate(((wg0_ref, wu0_ref, wd0_ref), (wg1_ref, wu1_ref, wd1_ref))):
            wgu_ref[k, :, :f] = wg_ref[...].astype(BF16)
            wgu_ref[k, :, f:] = wu_ref[...].astype(BF16)
            wd_ref[k] = wdn_ref[...].astype(BF16)

    @pl.when(i < n_tiles)
    def _():
        x = xa_ref[:, :d]
        gate_lo = xa_ref[:, d:d + 1]
        gate_hi = xa_ref[:, d + 1:d + 2]
        mrow = xa_ref[:, d + 2:d + 3]

        def pick(kind):
            base = 3 * kind
            return jnp.where(mrow < 0.5, mt_ref[base:base + 1, :],
                             jnp.where(mrow < 1.5, mt_ref[base + 1:base + 2, :], mt_ref[base + 2:base + 3, :]))

        h = _adaln(x, g_ref[...], pick(0), pick(1)).astype(BF16)
        out = None
        for k, gate in enumerate((gate_lo, gate_hi)):
            gu = jnp.dot(h, wgu_ref[k], preferred_element_type=F32)
            act = (_silu(gu[:, :f]) * gu[:, f:] * gate).astype(BF16)
            y = jnp.dot(act, wd_ref[k], preferred_element_type=F32)
            out = y if out is None else out + y
        o_ref[...] = x + pick(2) * out

    @pl.when(i >= n_tiles)
    def _():
        o_ref[...] = jnp.zeros_like(o_ref)


def _moe_experts(xa, plan, g, mtab, layer, wg, wu, wd):
    rows, da = xa.shape
    d = da - TAIL
    f = wg.shape[3]
    nt = rows // MOE_TILE

    def tile(i, plan_ref):
        return jnp.minimum(i, plan_ref[PLAN_N_TILES * LANES] - 1)

    def wspec(shape, row):
        return pl.BlockSpec((None, None, *shape), lambda i, p: (layer, p[row * LANES + tile(i, p)], 0, 0))

    grid_spec = pltpu.PrefetchScalarGridSpec(
        num_scalar_prefetch=1,
        grid=(nt,),
        in_specs=[pl.BlockSpec((MOE_TILE, da), lambda i, p: (tile(i, p), 0)),
                  pl.BlockSpec((1, d), lambda i, p: (0, 0)),
                  pl.BlockSpec(mtab.shape, lambda i, p: (0, 0)),
                  wspec((d, f), PLAN_E_LO), wspec((d, f), PLAN_E_LO), wspec((f, d), PLAN_E_LO),
                  wspec((d, f), PLAN_E_HI), wspec((d, f), PLAN_E_HI), wspec((f, d), PLAN_E_HI)],
        out_specs=pl.BlockSpec((MOE_TILE, d), lambda i, p: (i, 0)),
        scratch_shapes=[pltpu.VMEM((2, d, 2 * f), BF16), pltpu.VMEM((2, f, d), BF16)])
    return pl.pallas_call(
        _moe_expert_kernel,
        grid_spec=grid_spec,
        out_shape=jax.ShapeDtypeStruct((rows, d), F32),
        compiler_params=_params("arbitrary"),
        name="moe_experts",
    )(plan.reshape(-1), xa, g, mtab, wg, wu, wd, wg, wu, wd)


def _moe_gather_kernel(pos_ref, ys_ref, o_ref, sem):
    tm = o_ref.shape[0]

    for r in range(tm):
        _row_copy(ys_ref, pos_ref[0, r], o_ref, r, sem).start()
    pltpu.make_async_copy(ys_ref.at[pl.ds(0, tm)], o_ref, sem).wait()


def _moe_gather(ys, pos3, tile0, rows, tm):
    d = ys.shape[1]
    return pl.pallas_call(
        _moe_gather_kernel,
        grid=(rows // tm,),
        in_specs=[pl.BlockSpec((None, 1, tm), lambda i: (i + tile0, 0, 0), memory_space=pltpu.SMEM),
                  pl.BlockSpec(memory_space=pl.ANY)],
        out_specs=pl.BlockSpec((tm, d), lambda i: (i, 0)),
        out_shape=jax.ShapeDtypeStruct((rows, d), F32),
        scratch_shapes=[pltpu.SemaphoreType.DMA(())],
        compiler_params=_params("arbitrary"),
        name="moe_gather",
    )(pos3, ys)


def _moe(xs, streams, g, mod, layer, rw_t, rb_col, wg, wu, wd, buf, tm=512):
    tail, bucket = _moe_route(xs, streams, g, mod, layer, rw_t, rb_col)
    pos, plan = _moe_plan(bucket)
    pos3 = pos.reshape(-1, 1, tm)
    xa = _moe_scatter(xs, streams, tail, pos3, tm, buf)
    mtab = mod[layer, 3:6, :, 0, :].reshape(-1, xs[0].shape[1])
    ys = _moe_experts(xa, plan, g, mtab, layer, wg, wu, wd)
    out, tile0 = [], 0
    for st in streams:
        out.append(_moe_gather(ys, pos3, tile0, st.rows, tm))
        tile0 += st.rows // tm
    return out, xa


def _rope_tables(t_len):
    rows = t_len // GRID_W
    row_ids = jnp.repeat(jnp.arange(rows), GRID_W).astype(F32)
    col_ids = jnp.tile(jnp.arange(GRID_W), rows).astype(F32)
    n_freq = HEAD_DIM // 4
    inv = 1.0 / (ROPE_THETA ** (jnp.arange(n_freq, dtype=F32) / n_freq))
    ang = jnp.concatenate([row_ids[:, None] * inv, col_ids[:, None] * inv], axis=-1)
    cos, sin = jnp.cos(ang), jnp.sin(ang)
    cs = jnp.tile(jnp.concatenate([cos, cos], axis=-1), (1, LANES // HEAD_DIM))
    sn = jnp.tile(jnp.concatenate([-sin, sin], axis=-1), (1, LANES // HEAD_DIM))
    return cs, sn


def _pad_lanes(v):
    return jnp.pad(v.reshape(1, -1), ((0, 0), (0, LANES - v.size)))


def kernel(x_prompt, x_sample, c, c_ctx, cache_k, cache_v, state_ssm, mod_w, mod_b, norm1_g, norm2_g,
           attn_w_qkv, attn_q_norm_g, attn_k_norm_g, attn_w_o, ssd_in_proj, ssd_conv_w, ssd_conv_b,
           ssd_dt_bias, ssd_a_log, ssd_d, ssd_norm_g, ssd_out_proj, router_w, router_b, expert_w_gate,
           expert_w_up, expert_w_down):
    n_p, l_p, d = x_prompt.shape
    n_s, l_s, _ = x_sample.shape
    depth = mod_w.shape[0]
    past = cache_k.shape[2]
    nkv = N_KV_HEADS * HEAD_DIM
    assert d == D_MODEL and 1 + n_s == 3
    streams = (_Stream(n_p, l_p, 0, 0), _Stream(n_s, l_s, 1, 1))
    xs = [x_prompt.reshape(n_p * l_p, d), x_sample.reshape(n_s * l_s, d)]

    cond8 = jnp.concatenate([c_ctx[None], c, jnp.zeros((8 - 1 - n_s, d), F32)], axis=0)
    mod = _modulation(cond8, mod_w, mod_b)
    mod = mod[:, :1 + n_s].reshape(depth, 1 + n_s, 6, 1, d).transpose(0, 2, 1, 3, 4)

    rope = _rope_tables(l_s)
    rw_t = jnp.pad(router_w.T, ((0, LANES - N_EXPERTS), (0, 0))).astype(BF16)
    rb_col = jnp.pad(router_b, (0, LANES - N_EXPERTS)).reshape(LANES, 1)

    sorted_buf = _sorted_buffer(sum(st.rows for st in streams), d)
    new_k, new_v, new_ssm = [], [], []
    for layer in range(depth):
        i = layer // 2
        g1 = norm1_g[layer][None]
        if layer % 2 == 0:
            w_qkv = attn_w_qkv[i].astype(BF16)
            w_o = attn_w_o[i].astype(BF16)
            qg2 = jnp.tile(attn_q_norm_g[i], LANES // HEAD_DIM)[None]
            kg2 = jnp.tile(attn_k_norm_g[i], LANES // HEAD_DIM)[None]
            for si, st in enumerate(streams):
                if si == 0:
                    q, k2, v2, kc, vc = _qk_prep(xs[si], g1, mod, layer, st, w_qkv, qg2, kg2, emit_cache=True)
                    o = _attention(q, k2, v2, st.n_seq, st.seq_len)
                    new_k.append(kc.reshape(n_p, l_p, N_KV_HEADS, HEAD_DIM))
                    new_v.append(vc.reshape(n_p, l_p, N_KV_HEADS, HEAD_DIM))
                else:
                    q, k2, v2 = _qk_prep(xs[si], g1, mod, layer, st, w_qkv, qg2, kg2, tables=rope)
                    kc2, vc2 = _kv_expand(cache_k[:, i].reshape(n_s * past, nkv),
                                          cache_v[:, i].reshape(n_s * past, nkv))
                    o = _attention(q, k2, v2, st.n_seq, st.seq_len, ctx=(kc2, vc2, past))
                xs[si] = _mm_res(o, w_o, xs[si], mod, layer, st)
        else:
            w_in = ssd_in_proj[i]
            nzx = D_INNER + CONV_DIM
            w_zx = w_in[:, :nzx].astype(BF16)
            w_dt = jnp.pad(w_in[:, nzx:], ((0, 0), (0, LANES - 2 * SSD_HEADS))).astype(BF16)
            w_out = ssd_out_proj[i].astype(BF16)
            dtb = _pad_lanes(ssd_dt_bias[i])
            alog = _pad_lanes(ssd_a_log[i])
            dskip = jnp.repeat(ssd_d[i], SSD_HEAD_DIM)[None]
            cw, cb = ssd_conv_w[i], ssd_conv_b[i][None]
            for si, st in enumerate(streams):
                z, xbc, dt = _in_proj_conv(xs[si], g1, mod, layer, st, w_zx, w_dt, cw, cb)
                init = None if si == 0 else state_ssm[:, i].reshape(n_s, 2, D_INNER, D_STATE)
                y_f, s_f = _ssd_scan(xbc, dt, dtb, alog, st, rev=False, init=init)
                yg, s_b = _ssd_scan(xbc, dt, dtb, alog, st, rev=True, init=init, final=(y_f, z, dskip))
                if si == 0:
                    new_ssm.append(jnp.stack([s_f, s_b], axis=1).reshape(
                        n_p, 2, SSD_HEADS, SSD_HEAD_DIM, D_STATE))
                xs[si] = _mm_res(yg, w_out, xs[si], mod, layer, st, norm_g=ssd_norm_g[i][None])
        xs, sorted_buf = _moe(xs, streams, norm2_g[layer][None], mod, layer, rw_t, rb_col,
                              expert_w_gate, expert_w_up, expert_w_down, sorted_buf)

    return (xs[0].reshape(n_p, l_p, d), xs[1].reshape(n_s, l_s, d),
            jnp.stack(new_k, axis=1), jnp.stack(new_v, axis=1), jnp.stack(new_ssm, axis=1))
```

```python
import functools
import math
from typing import NamedTuple

import jax
import jax.numpy as jnp
from jax import lax
from jax.experimental import pallas as pl
from jax.experimental.pallas import tpu as pltpu

F32 = jnp.float32
BF16 = jnp.bfloat16

D_MODEL = 1024
GRID_W = 64
N_HEADS = 16
N_KV_HEADS = 4
HEAD_DIM = 64
KV_REP = N_HEADS // N_KV_HEADS
ROPE_THETA = 10000.0
D_INNER = 2048
SSD_HEAD_DIM = 64
SSD_HEADS = 32
SSD_GROUPS = 4
SSD_REP = SSD_HEADS // SSD_GROUPS
D_STATE = 128
D_CONV = 5
CHUNK = 128
CONV_DIM = D_INNER + 2 * SSD_GROUPS * D_STATE
N_EXPERTS = 16
N_EXPERT_GROUPS = 4
EXPERTS_PER_GROUP = 4
D_FF_EXPERT = 256
EPS = 1e-6

LANES = 128
VMEM_LIMIT = 56 * 1024 * 1024


def _params(*sem):
    return pltpu.CompilerParams(dimension_semantics=sem, vmem_limit_bytes=VMEM_LIMIT)


def _silu(x):
    return x * jax.nn.sigmoid(x)


class _Stream(NamedTuple):
    n_seq: int
    seq_len: int
    mod_base: int
    mod_step: int

    @property
    def rows(self):
        return self.n_seq * self.seq_len


def _mod_kernel(c_ref, w_ref, b_ref, o_ref):
    s = _silu(c_ref[...])
    o_ref[...] = jnp.dot(s.astype(BF16), w_ref[...].astype(BF16),
                         preferred_element_type=F32) + b_ref[...]


def _modulation(cond8, mod_w, mod_b):
    depth, d, n = mod_w.shape
    tn = 1536
    return pl.pallas_call(
        _mod_kernel,
        grid=(depth, n // tn),
        in_specs=[pl.BlockSpec((8, d), lambda l, j: (0, 0)),
                  pl.BlockSpec((None, d, tn), lambda l, j: (l, 0, j)),
                  pl.BlockSpec((None, 1, tn), lambda l, j: (l, 0, j))],
        out_specs=pl.BlockSpec((None, 8, tn), lambda l, j: (l, 0, j)),
        out_shape=jax.ShapeDtypeStruct((depth, 8, n), F32),
        compiler_params=_params("parallel", "parallel"),
        name="modulation",
    )(cond8, mod_w, mod_b.reshape(depth, 1, n))


def _mod_spec(layer, chunk, tm, st):
    def imap(i, *_):
        return (layer, chunk, st.mod_base + st.mod_step * (i * tm // st.seq_len), 0, 0)
    return pl.BlockSpec((None, None, None, 1, D_MODEL), imap)


def _adaln(x, g, sh, sc):
    ms = jnp.mean(x * x, axis=-1, keepdims=True)
    y = x * lax.rsqrt(ms + EPS) * g
    return y * (1 + sc) + sh


def _mm_res_kernel(a_ref, w_ref, r_ref, gate_ref, o_ref):
    y = jnp.dot(a_ref[...], w_ref[...], preferred_element_type=F32)
    o_ref[...] = r_ref[...] + gate_ref[...] * y


def _norm_mm_res_kernel(a_ref, ng_ref, w_ref, r_ref, gate_ref, o_ref):
    a = a_ref[...]
    ms = jnp.mean(a * a, axis=-1, keepdims=True)
    an = (a * lax.rsqrt(ms + EPS) * ng_ref[...]).astype(BF16)
    y = jnp.dot(an, w_ref[...], preferred_element_type=F32)
    o_ref[...] = r_ref[...] + gate_ref[...] * y


def _mm_res(a, w, res, mod, layer, st, norm_g=None, tm=512):
    t, k = a.shape
    n = w.shape[1]
    a_spec = pl.BlockSpec((tm, k), lambda i: (i, 0))
    w_spec = pl.BlockSpec((k, n), lambda i: (0, 0))
    r_spec = pl.BlockSpec((tm, n), lambda i: (i, 0))
    gate_spec = _mod_spec(layer, 2, tm, st)
    if norm_g is None:
        kern, specs, args = _mm_res_kernel, [a_spec, w_spec, r_spec, gate_spec], (a, w, res, mod)
    else:
        kern = _norm_mm_res_kernel
        specs = [a_spec, pl.BlockSpec((1, k), lambda i: (0, 0)), w_spec, r_spec, gate_spec]
        args = (a, norm_g, w, res, mod)
    return pl.pallas_call(
        kern,
        grid=(t // tm,),
        in_specs=specs,
        out_specs=pl.BlockSpec((tm, n), lambda i: (i, 0)),
        out_shape=jax.ShapeDtypeStruct((t, n), F32),
        compiler_params=_params("parallel"),
        name="mm_res",
    )(*args)


def _lane_iota(rows):
    return lax.broadcasted_iota(jnp.int32, (rows, LANES), 1)


def _head_pair_norm(c, g2):
    lo = _lane_iota(c.shape[0]) < HEAD_DIM
    cc = c * c
    s_lo = jnp.sum(jnp.where(lo, cc, 0.0), axis=-1, keepdims=True)
    s_hi = jnp.sum(jnp.where(lo, 0.0, cc), axis=-1, keepdims=True)
    r = jnp.where(lo, lax.rsqrt(s_lo / HEAD_DIM + EPS), lax.rsqrt(s_hi / HEAD_DIM + EPS))
    return c * r * g2


def _rope_pair(y, cs, sn):
    first = (_lane_iota(y.shape[0]) % HEAD_DIM) < HEAD_DIM // 2
    partner = jnp.where(first, pltpu.roll(y, LANES - HEAD_DIM // 2, 1), pltpu.roll(y, HEAD_DIM // 2, 1))
    return y * cs + partner * sn


def _store_v_split(ref, j, c):
    lane = _lane_iota(c.shape[0])
    lo = lane < HEAD_DIM
    a_lo = jnp.where(lo, c, jnp.where(lane == HEAD_DIM, 1.0, 0.0))
    b_hi = jnp.where(lo, jnp.where(lane == 0, 1.0, 0.0), c)
    ref[2 * j, 0] = a_lo.astype(BF16)
    ref[2 * j, 1] = pltpu.roll(a_lo, HEAD_DIM, 1).astype(BF16)
    ref[2 * j + 1, 0] = pltpu.roll(b_hi, HEAD_DIM, 1).astype(BF16)
    ref[2 * j + 1, 1] = b_hi.astype(BF16)


def _store_kt_split(ref, j, c):
    t = c.T
    a, b = t[:HEAD_DIM], t[HEAD_DIM:]
    z = jnp.zeros_like(a)
    ref[2 * j, 0] = jnp.concatenate([a, z], axis=0).astype(BF16)
    ref[2 * j, 1] = jnp.concatenate([z, a], axis=0).astype(BF16)
    ref[2 * j + 1, 0] = jnp.concatenate([b, z], axis=0).astype(BF16)
    ref[2 * j + 1, 1] = jnp.concatenate([z, b], axis=0).astype(BF16)


def _qk_prep_kernel(*refs, rope, emit_cache):
    it = iter(refs)
    qkv_ref, qg_ref, kg_ref = next(it), next(it), next(it)
    cs_ref, sn_ref = (next(it), next(it)) if rope else (None, None)
    q_ref, k2_ref, v2_ref = next(it), next(it), next(it)
    kc_ref, vc_ref = (next(it), next(it)) if emit_cache else (None, None)

    nq = N_HEADS * HEAD_DIM
    nkv = N_KV_HEADS * HEAD_DIM
    scale = HEAD_DIM ** -0.5 * math.log2(math.e)
    for j in range(nq // LANES):
        y = _head_pair_norm(qkv_ref[:, j * LANES:(j + 1) * LANES], qg_ref[...])
        if rope:
            y = _rope_pair(y, cs_ref[...], sn_ref[...])
        q_ref[:, j * LANES:(j + 1) * LANES] = (y * scale).astype(BF16)
    for j in range(nkv // LANES):
        y = _head_pair_norm(qkv_ref[:, nq + j * LANES:nq + (j + 1) * LANES], kg_ref[...])
        if rope:
            y = _rope_pair(y, cs_ref[...], sn_ref[...])
        if emit_cache:
            kc_ref[:, j * LANES:(j + 1) * LANES] = y
        _store_kt_split(k2_ref, j, y)
        v = qkv_ref[:, nq + nkv + j * LANES:nq + nkv + (j + 1) * LANES]
        if emit_cache:
            vc_ref[:, j * LANES:(j + 1) * LANES] = v
        _store_v_split(v2_ref, j, v)


def _kv_shapes(rows):
    return (jax.ShapeDtypeStruct((N_KV_HEADS, 2, LANES, rows), BF16),
            jax.ShapeDtypeStruct((N_KV_HEADS, 2, rows, LANES), BF16))


def _kv_specs(tm):
    return (pl.BlockSpec((N_KV_HEADS, 2, LANES, tm), lambda i: (0, 0, 0, i)),
            pl.BlockSpec((N_KV_HEADS, 2, tm, LANES), lambda i: (0, 0, i, 0)))


def _qkv_kernel(x_ref, g_ref, sh_ref, sc_ref, w_ref, *refs, rope, emit_cache):
    qkv_ref = refs[-1]
    h = _adaln(x_ref[...], g_ref[...], sh_ref[...], sc_ref[...]).astype(BF16)
    qkv_ref[...] = jnp.dot(h, w_ref[...], preferred_element_type=F32)
    _qk_prep_kernel(qkv_ref, *refs[:-1], rope=rope, emit_cache=emit_cache)


def _qk_prep(x, g, mod, layer, st, w, qg2, kg2, tables=None, emit_cache=False, tm=512):
    rows, d = x.shape
    n = w.shape[1]
    nq = N_HEADS * HEAD_DIM
    nkv = N_KV_HEADS * HEAD_DIM
    rope = tables is not None
    in_specs = [pl.BlockSpec((tm, d), lambda i: (i, 0)),
                pl.BlockSpec((1, d), lambda i: (0, 0)),
                _mod_spec(layer, 0, tm, st),
                _mod_spec(layer, 1, tm, st),
                pl.BlockSpec((d, n), lambda i: (0, 0)),
                pl.BlockSpec((1, LANES), lambda i: (0, 0)),
                pl.BlockSpec((1, LANES), lambda i: (0, 0))]
    args = [x, g, mod, mod, w, qg2, kg2]
    if rope:
        nt = tables[0].shape[0] // tm
        in_specs += [pl.BlockSpec((tm, LANES), lambda i: (i % nt, 0))] * 2
        args += list(tables)
    out_shape = [jax.ShapeDtypeStruct((rows, nq), BF16), *_kv_shapes(rows)]
    out_specs = [pl.BlockSpec((tm, nq), lambda i: (i, 0)), *_kv_specs(tm)]
    if emit_cache:
        out_shape += [jax.ShapeDtypeStruct((rows, nkv), F32)] * 2
        out_specs += [pl.BlockSpec((tm, nkv), lambda i: (i, 0))] * 2
    return pl.pallas_call(
        functools.partial(_qkv_kernel, rope=rope, emit_cache=emit_cache),
        grid=(rows // tm,),
        in_specs=in_specs,
        out_specs=out_specs,
        out_shape=out_shape,
        scratch_shapes=[pltpu.VMEM((tm, n), F32)],
        compiler_params=_params("parallel"),
        name="qkv_prep",
    )(*args)


def _kv_expand_kernel(k_ref, v_ref, k2_ref, v2_ref):
    for j in range(N_KV_HEADS * HEAD_DIM // LANES):
        _store_kt_split(k2_ref, j, k_ref[:, j * LANES:(j + 1) * LANES])
        _store_v_split(v2_ref, j, v_ref[:, j * LANES:(j + 1) * LANES])


def _kv_expand(k, v, tm=512):
    rows, nkv = k.shape
    return pl.pallas_call(
        _kv_expand_kernel,
        grid=(rows // tm,),
        in_specs=[pl.BlockSpec((tm, nkv), lambda i: (i, 0))] * 2,
        out_specs=list(_kv_specs(tm)),
        out_shape=list(_kv_shapes(rows)),
        compiler_params=_params("parallel"),
        name="kv_expand",
    )(k, v)


def _attn_kernel(*refs, tk, n_tiles, has_ctx):
    if has_ctx:
        q_ref, k_ref, v_ref, kc_ref, vc_ref, o_ref = refs
    else:
        q_ref, k_ref, v_ref, o_ref = refs
        kc_ref = vc_ref = None
    tq = q_ref.shape[0]
    lo = _lane_iota(tq) < HEAD_DIM
    tiles = [(k_ref, v_ref, t) for t in range(n_tiles)]
    if has_ctx:
        tiles += [(kc_ref, vc_ref, t) for t in range(vc_ref.shape[2] // tk)]
    for gi in range(k_ref.shape[0]):
        for pair in range(KV_REP // 2):
            c0 = (gi * (KV_REP // 2) + pair) * LANES
            qp = q_ref[:, c0:c0 + LANES]
            m = [jnp.full((tq, 1), -jnp.inf, F32)] * 2
            acc = [jnp.zeros((tq, LANES), F32)] * 2
            for kr, vr, t in tiles:
                for half in range(2):
                    s = jnp.dot(qp, kr[gi, half, :, t * tk:(t + 1) * tk], preferred_element_type=F32)
                    m_new = jnp.maximum(m[half], jnp.max(s, axis=-1, keepdims=True))
                    p = jnp.exp2(s - m_new).astype(BF16)
                    acc[half] = jnp.exp2(m[half] - m_new) * acc[half] + jnp.dot(
                        p, vr[gi, half, t * tk:(t + 1) * tk, :], preferred_element_type=F32)
                    m[half] = m_new
            out_lo = acc[0] / acc[0][:, HEAD_DIM:HEAD_DIM + 1]
            out_hi = acc[1] / acc[1][:, 0:1]
            o_ref[:, c0:c0 + LANES] = jnp.where(lo, out_lo, out_hi).astype(BF16)


def _attention(q, k2, v2, n_batch, t_len, ctx=None, tq=1024, tk=512, groups=1):
    tk = min(tk, t_len)
    tq = min(tq, t_len)
    nq = t_len // tq
    gw = groups * KV_REP * HEAD_DIM
    q_spec = pl.BlockSpec((tq, gw), lambda b, g, i: (b * nq + i, g))
    in_specs = [q_spec,
                pl.BlockSpec((groups, 2, LANES, t_len), lambda b, g, i: (g, 0, 0, b)),
                pl.BlockSpec((groups, 2, t_len, LANES), lambda b, g, i: (g, 0, b, 0))]
    args = [q, k2, v2]
    if ctx is not None:
        kc2, vc2, ctx_len = ctx
        in_specs += [pl.BlockSpec((groups, 2, LANES, ctx_len), lambda b, g, i: (g, 0, 0, b)),
                     pl.BlockSpec((groups, 2, ctx_len, LANES), lambda b, g, i: (g, 0, b, 0))]
        args += [kc2, vc2]
    return pl.pallas_call(
        functools.partial(_attn_kernel, tk=tk, n_tiles=t_len // tk, has_ctx=ctx is not None),
        grid=(n_batch, N_KV_HEADS // groups, nq),
        in_specs=in_specs,
        out_specs=q_spec,
        out_shape=jax.ShapeDtypeStruct(q.shape, BF16),
        compiler_params=_params("parallel", "parallel", "arbitrary"),
        name="attention",
    )(*args)


HALO = 16


def _in_proj_kernel(x_ref, xa_ref, xb_ref, g_ref, sh_ref, sc_ref, w_ref, wdt_ref, cw_ref, cb_ref,
                    z_ref, xbc_ref, dt_ref, h_ref, u_ref, *, n_z_tiles, seq_len):
    i, j = pl.program_id(0), pl.program_id(1)
    tm = x_ref.shape[0]

    @pl.when(j == 0)
    def _():
        def norm(ref):
            return _adaln(ref[...], g_ref[...], sh_ref[...], sc_ref[...]).astype(BF16)
        h_ref[0:HALO] = norm(xa_ref)
        h = norm(x_ref)
        h_ref[HALO:HALO + tm] = h
        h_ref[HALO + tm:] = norm(xb_ref)
        dt_ref[...] = jnp.dot(h, wdt_ref[...], preferred_element_type=F32)

    @pl.when(j < n_z_tiles)
    def _():
        z_ref[...] = jnp.dot(h_ref[HALO:HALO + tm], w_ref[...], preferred_element_type=F32)

    @pl.when(j >= n_z_tiles)
    def _():
        u = jnp.dot(h_ref[...], w_ref[...], preferred_element_type=F32)
        keep_a = ((i * tm) % seq_len != 0).astype(F32)
        keep_b = (((i + 1) * tm) % seq_len != 0).astype(F32)
        u_ref[0:HALO] = u[0:HALO] * keep_a
        u_ref[HALO:HALO + tm] = u[HALO:HALO + tm]
        u_ref[HALO + tm:] = u[HALO + tm:] * keep_b
        acc = cb_ref[...]
        for k in range(D_CONV):
            d = k - D_CONV // 2
            acc = acc + u_ref[HALO + d:HALO + d + tm, :] * cw_ref[k:k + 1, :]
        xbc_ref[...] = _silu(acc)


def _in_proj_conv(x, g, mod, layer, st, w_zx, w_dt, conv_w, conv_b, tm=512, tn=1024):
    t, d = x.shape
    tm = min(tm, st.seq_len)
    assert st.seq_len % tm == 0 and D_INNER % tn == 0 and CONV_DIM % tn == 0
    nzt, nct = D_INNER // tn, CONV_DIM // tn
    hb = tm // HALO
    last_halo = t // HALO - 1
    return pl.pallas_call(
        functools.partial(_in_proj_kernel, n_z_tiles=nzt, seq_len=st.seq_len),
        grid=(t // tm, nzt + nct),
        in_specs=[pl.BlockSpec((tm, d), lambda i, j: (i, 0)),
                  pl.BlockSpec((HALO, d), lambda i, j: (jnp.maximum(i * hb - 1, 0), 0)),
                  pl.BlockSpec((HALO, d), lambda i, j: (jnp.minimum((i + 1) * hb, last_halo), 0)),
                  pl.BlockSpec((1, d), lambda i, j: (0, 0)),
                  _mod_spec(layer, 0, tm, st),
                  _mod_spec(layer, 1, tm, st),
                  pl.BlockSpec((d, tn), lambda i, j: (0, j)),
                  pl.BlockSpec((d, LANES), lambda i, j: (0, 0)),
                  pl.BlockSpec((D_CONV, tn), lambda i, j: (0, jnp.maximum(j - nzt, 0))),
                  pl.BlockSpec((1, tn), lambda i, j: (0, jnp.maximum(j - nzt, 0)))],
        out_specs=[pl.BlockSpec((tm, tn), lambda i, j: (i, jnp.minimum(j, nzt - 1))),
                   pl.BlockSpec((tm, tn), lambda i, j: (i, jnp.maximum(j - nzt, 0))),
                   pl.BlockSpec((tm, LANES), lambda i, j: (i, 0))],
        out_shape=[jax.ShapeDtypeStruct((t, D_INNER), F32),
                   jax.ShapeDtypeStruct((t, CONV_DIM), F32),
                   jax.ShapeDtypeStruct((t, LANES), F32)],
        scratch_shapes=[pltpu.VMEM((tm + 2 * HALO, d), BF16), pltpu.VMEM((tm + 2 * HALO, tn), F32)],
        compiler_params=_params("parallel", "arbitrary"),
        name="in_proj_conv",
    )(x, x, x, g, mod, mod, w_zx, w_dt, conv_w, conv_b)


def _cumsum_rows(v, reverse):
    n = v.shape[0]
    row = lax.broadcasted_iota(jnp.int32, v.shape, 0)
    k = 1
    while k < n:
        if reverse:
            v = v + jnp.where(row < n - k, pltpu.roll(v, n - k, 0), 0.0)
        else:
            v = v + jnp.where(row >= k, pltpu.roll(v, k, 0), 0.0)
        k *= 2
    return v


def _ssd_kernel(*refs, rev, final_pass, has_init, cps):
    it = iter(refs)
    x_ref, b_ref, c_ref, dt_ref, dtb_ref, alog_ref = [next(it) for _ in range(6)]
    init_ref = next(it) if has_init else None
    if final_pass:
        y0_ref, z_ref, dskip_ref = next(it), next(it), next(it)
    y_ref, fin_ref, st_ref = next(it), next(it), next(it)

    ci = pl.program_id(1)
    n_blk = D_INNER // LANES

    @pl.when(ci == 0)
    def _():
        if has_init:
            for k in range(n_blk):
                st_ref[:, k * LANES:(k + 1) * LANES] = init_ref[k * LANES:(k + 1) * LANES, :].T
        else:
            st_ref[...] = jnp.zeros_like(st_ref)

    dirn = 1 if rev else 0
    neg_a = -jnp.exp(alog_ref[...])
    row = lax.broadcasted_iota(jnp.int32, (CHUNK, CHUNK), 0)
    col = lax.broadcasted_iota(jnp.int32, (CHUNK, CHUNK), 1)
    causal = (col >= row) if rev else (col <= row)
    lo = _lane_iota(CHUNK) < SSD_HEAD_DIM
    last = 0 if rev else CHUNK - 1

    for cc in (reversed(range(cps)) if rev else range(cps)):
        rows = slice(cc * CHUNK, (cc + 1) * CHUNK)
        dtv = jax.nn.softplus(dt_ref[rows, :] + dtb_ref[...])
        acum = _cumsum_rows(dtv * neg_a, rev) * math.log2(math.e)
        acum_t = acum.T
        dtv_t = dtv.T
        w_t = dtv_t * jnp.exp2(acum_t[:, last:last + 1] - acum_t)
        e_tot = jnp.exp2(acum[last:last + 1, :])
        src_t = acum_t - jnp.log2(dtv_t)
        for g in range(SSD_GROUPS):
            bg = b_ref[rows, g * D_STATE:(g + 1) * D_STATE]
            cg = c_ref[rows, g * D_STATE:(g + 1) * D_STATE]
            cb = lax.dot_general(cg.astype(BF16), bg.astype(BF16), (((1,), (1,)), ((), ())),
                                 preferred_element_type=F32)
            bg_t = bg.T
            for pp in range(SSD_REP // 2):
                h0 = g * SSD_REP + 2 * pp
                i0 = dirn * SSD_HEADS + h0
                cols = slice(h0 * SSD_HEAD_DIM, (h0 + 2) * SSD_HEAD_DIM)
                xp = x_ref[rows, cols]
                st_pair = st_ref[:, cols]
                halves = ((jnp.where(lo, xp, 0.0).astype(BF16), jnp.where(lo, st_pair, 0.0).astype(BF16)),
                          (jnp.where(lo, 0.0, xp).astype(BF16), jnp.where(lo, 0.0, st_pair).astype(BF16)))
                y = new = None
                for hh, (xh, sth) in enumerate(halves):
                    i = i0 + hh
                    a_col = jnp.broadcast_to(acum[:, i:i + 1], (CHUNK, CHUNK))
                    lmat = jnp.exp2(jnp.where(causal, a_col - src_t[i:i + 1, :], -jnp.inf))
                    m = (cb * lmat).astype(BF16)
                    ce = (cg * jnp.exp2(a_col)).astype(BF16)
                    yh = jnp.dot(jnp.concatenate([m, ce], axis=1), jnp.concatenate([xh, sth], axis=0),
                                 preferred_element_type=F32)
                    nh = jnp.dot((bg_t * w_t[i:i + 1, :]).astype(BF16), xh, preferred_element_type=F32)
                    y = yh if y is None else y + yh
                    new = nh if new is None else new + nh
                dec = jnp.where(lo[:1], e_tot[:, i0:i0 + 1], e_tot[:, i0 + 1:i0 + 2])
                st_ref[:, cols] = st_pair * dec + new
                if final_pass:
                    y = (y0_ref[rows, cols] + y + xp * dskip_ref[:, cols]) * _silu(z_ref[rows, cols])
                y_ref[rows, cols] = y

    @pl.when(ci == pl.num_programs(1) - 1)
    def _():
        for k in range(n_blk):
            fin_ref[k * LANES:(k + 1) * LANES, :] = st_ref[:, k * LANES:(k + 1) * LANES].T


def _ssd_scan(xbc, dt, dtb, alog, st, rev, init=None, final=None, cps=4):
    cps = min(cps, st.seq_len // CHUNK)
    blk = cps * CHUNK
    nb = st.seq_len // blk
    dirn = 1 if rev else 0

    def tok(s, c):
        return (s * nb + (nb - 1 - c if rev else c), 0)

    bc_w = SSD_GROUPS * D_STATE
    b_col = D_INNER // bc_w
    tok_spec = pl.BlockSpec((blk, D_INNER), tok)
    in_specs = [tok_spec,
                pl.BlockSpec((blk, bc_w), lambda s, c: (tok(s, c)[0], b_col)),
                pl.BlockSpec((blk, bc_w), lambda s, c: (tok(s, c)[0], b_col + 1)),
                pl.BlockSpec((blk, LANES), tok),
                pl.BlockSpec((1, LANES), lambda s, c: (0, 0)),
                pl.BlockSpec((1, LANES), lambda s, c: (0, 0))]
    args = [xbc, xbc, xbc, dt, dtb, alog]
    if init is not None:
        in_specs.append(pl.BlockSpec((None, None, D_INNER, D_STATE), lambda s, c: (s, dirn, 0, 0)))
        args.append(init)
    if final is not None:
        y0, zx, dskip = final
        in_specs += [tok_spec, tok_spec, pl.BlockSpec((1, D_INNER), lambda s, c: (0, 0))]
        args += [y0, zx, dskip]
    return pl.pallas_call(
        functools.partial(_ssd_kernel, rev=rev, final_pass=final is not None,
                          has_init=init is not None, cps=cps),
        grid=(st.n_seq, nb),
        in_specs=in_specs,
        out_specs=[tok_spec, pl.BlockSpec((None, D_INNER, D_STATE), lambda s, c: (s, 0, 0))],
        out_shape=[jax.ShapeDtypeStruct((st.rows, D_INNER), F32),
                   jax.ShapeDtypeStruct((st.n_seq, D_INNER, D_STATE), F32)],
        scratch_shapes=[pltpu.VMEM((D_STATE, D_INNER), F32)],
        compiler_params=_params("parallel", "arbitrary"),
        name="ssd_scan",
    )(*args)


MOE_TILE = 512
PAIRS_PER_GROUP = EXPERTS_PER_GROUP * (EXPERTS_PER_GROUP - 1) // 2
N_BUCKETS = N_EXPERT_GROUPS * PAIRS_PER_GROUP
BUCKET_ROWS = 32
TAIL = LANES


def _route_picks(biased_t, scores_t):
    ng, ne = N_EXPERT_GROUPS, EXPERTS_PER_GROUP
    b = [[biased_t[j * ne + k:j * ne + k + 1, :] for k in range(ne)] for j in range(ng)]
    s = [[scores_t[j * ne + k:j * ne + k + 1, :] for k in range(ne)] for j in range(ng)]
    gs = []
    for j in range(ng):
        best = None
        for k1 in range(ne):
            for k2 in range(k1 + 1, ne):
                ps = b[j][k1] + b[j][k2]
                best = ps if best is None else jnp.maximum(best, ps)
        gs.append(best)
    sel = jnp.zeros_like(gs[0], jnp.int32)
    best = gs[0]
    for j in range(1, ng):
        gt = gs[j] > best
        best = jnp.where(gt, gs[j], best)
        sel = jnp.where(gt, j, sel)
    in_grp = [sel == j for j in range(ng)]

    def pick(vals):
        out = []
        for k in range(ne):
            v = vals[ng - 1][k]
            for j in range(ng - 2, -1, -1):
                v = jnp.where(in_grp[j], vals[j][k], v)
            out.append(v)
        return out

    yb, ys = pick(b), pick(s)

    def argmax_first(vals):
        bv, bi = vals[0], jnp.zeros_like(sel)
        for k in range(1, ne):
            gt = vals[k] > bv
            bv = jnp.where(gt, vals[k], bv)
            bi = jnp.where(gt, k, bi)
        return bi

    i1 = argmax_first(yb)
    i2 = argmax_first([jnp.where(i1 == k, -jnp.inf, yb[k]) for k in range(ne)])

    def take(vals, idx):
        v = vals[ne - 1]
        for k in range(ne - 2, -1, -1):
            v = jnp.where(idx == k, vals[k], v)
        return v

    w1, w2 = take(ys, i1), take(ys, i2)
    tot = w1 + w2
    return sel, i1, i2, w1 / tot, w2 / tot


def _two_stream_specs(streams, tm, width):
    n0 = streams[0].rows // tm
    return (pl.BlockSpec((tm, width), lambda i, *_: (jnp.minimum(i, n0 - 1), 0)),
            pl.BlockSpec((tm, width), lambda i, *_: (jnp.maximum(i - n0, 0), 0)))


def _moe_route_kernel(x0_ref, x1_ref, g_ref, sh0_ref, sc0_ref, sh1_ref, sc1_ref, rw_ref, rb_ref,
                      tail_ref, bucket_ref, tt_ref, *, n0, mod_of_tile):
    i = pl.program_id(0)
    first = i < n0
    x = jnp.where(first, x0_ref[...], x1_ref[...])
    sh = jnp.where(first, sh0_ref[...], sh1_ref[...])
    sc = jnp.where(first, sc0_ref[...], sc1_ref[...])
    h = _adaln(x, g_ref[...], sh, sc).astype(BF16)
    logits_t = lax.dot_general(rw_ref[...], h, (((1,), (1,)), ((), ())),
                               preferred_element_type=F32)
    scores_t = jax.nn.sigmoid(logits_t)
    sel, i1, i2, g1, g2 = _route_picks(scores_t + rb_ref[...], scores_t)
    first_lo = i1 < i2
    klo = jnp.where(first_lo, i1, i2)
    khi = jnp.where(first_lo, i2, i1)
    pair = jnp.where(klo == 0, 0, jnp.where(klo == 1, 3, 5)) + (khi - klo - 1)
    bucket_ref[...] = sel * PAIRS_PER_GROUP + pair
    tt_ref[...] = jnp.zeros_like(tt_ref)
    tt_ref[0:1, :] = jnp.where(first_lo, g1, g2)
    tt_ref[1:2, :] = jnp.where(first_lo, g2, g1)
    tt_ref[2:3, :] = jnp.full(g1.shape, 1.0, F32) * mod_of_tile(i).astype(F32)
    tail_ref[...] = tt_ref[...].T


def _moe_route(xs, streams, g, mod, layer, rw_t, rb_col, tm=1024):
    d = xs[0].shape[1]
    t = sum(st.rows for st in streams)
    n0 = streams[0].rows // tm
    s0, s1 = streams

    def mod_of_tile(i):
        row0 = s0.mod_base + s0.mod_step * (i * tm // s0.seq_len)
        row1 = s1.mod_base + s1.mod_step * ((i - n0) * tm // s1.seq_len)
        return jnp.where(i < n0, row0, row1)

    def mspec(chunk, st, shift):
        def imap(i):
            return (layer, chunk, st.mod_base + st.mod_step * (jnp.maximum(i - shift, 0) * tm // st.seq_len), 0, 0)
        return pl.BlockSpec((None, None, None, 1, d), imap)

    x_specs = _two_stream_specs(streams, tm, d)
    return pl.pallas_call(
        functools.partial(_moe_route_kernel, n0=n0, mod_of_tile=mod_of_tile),
        grid=(t // tm,),
        in_specs=[*x_specs,
                  pl.BlockSpec((1, d), lambda i: (0, 0)),
                  mspec(3, s0, 0), mspec(4, s0, 0), mspec(3, s1, n0), mspec(4, s1, n0),
                  pl.BlockSpec((LANES, d), lambda i: (0, 0)),
                  pl.BlockSpec((LANES, 1), lambda i: (0, 0))],
        out_specs=[pl.BlockSpec((tm, TAIL), lambda i: (i, 0)),
                   pl.BlockSpec((1, tm), lambda i: (0, i))],
        out_shape=[jax.ShapeDtypeStruct((t, TAIL), F32), jax.ShapeDtypeStruct((1, t), jnp.int32)],
        scratch_shapes=[pltpu.VMEM((LANES, tm), F32)],
        compiler_params=_params("parallel"),
        name="moe_route",
    )(*xs, g, mod, mod, mod, mod, rw_t, rb_col)


PLAN_E_LO, PLAN_E_HI, PLAN_N_TILES, PLAN_ROWS = 0, 1, 2, 8


def _moe_plan_kernel(bucket_ref, pos_ref, plan_ref, *, blk):
    n_tok = bucket_ref.shape[1]
    bid = lax.broadcasted_iota(jnp.int32, (BUCKET_ROWS, blk), 0)
    r = lax.broadcasted_iota(jnp.int32, (blk, blk), 0)
    c = lax.broadcasted_iota(jnp.int32, (blk, blk), 1)
    upper = (r <= c).astype(BF16)

    def onehot(k):
        return (bucket_ref[:, k * blk:(k + 1) * blk] == bid).astype(F32)

    counts = jnp.zeros((BUCKET_ROWS, 1), F32)
    for k in range(n_tok // blk):
        counts = counts + jnp.sum(onehot(k), axis=1, keepdims=True)
    padded = jnp.floor((counts + (MOE_TILE - 1)) * (1.0 / MOE_TILE)) * MOE_TILE
    padded_b = jnp.broadcast_to(padded, (BUCKET_ROWS, LANES))
    starts_b = _cumsum_rows(padded_b, False) - padded_b
    starts = starts_b[:, 0:1]

    carry = jnp.zeros((BUCKET_ROWS, 1), F32)
    for k in range(n_tok // blk):
        oh = onehot(k)
        incl = jnp.dot(oh.astype(BF16), upper, preferred_element_type=F32)
        slot = jnp.sum(oh * (starts + carry + incl - 1.0), axis=0, keepdims=True)
        pos_ref[:, k * blk:(k + 1) * blk] = slot.astype(jnp.int32)
        carry = carry + incl[:, blk - 1:blk]

    tile0 = (lax.broadcasted_iota(jnp.int32, (BUCKET_ROWS, LANES), 1) * MOE_TILE).astype(F32)
    inside = (tile0 >= starts_b) & (tile0 < starts_b + padded_b)
    brow = lax.broadcasted_iota(jnp.int32, (BUCKET_ROWS, LANES), 0).astype(F32)
    tb = jnp.sum(jnp.where(inside, brow, 0.0), axis=0, keepdims=True).astype(jnp.int32)
    n_tiles = jnp.sum(jnp.sum(inside.astype(F32), axis=0, keepdims=True),
                      axis=1, keepdims=True).astype(jnp.int32)
    one = jnp.ones_like(tb)
    grp = jnp.where(tb >= 3 * PAIRS_PER_GROUP, 3 * one,
                    jnp.where(tb >= 2 * PAIRS_PER_GROUP, 2 * one, jnp.where(tb >= PAIRS_PER_GROUP, one, 0 * one)))
    p = tb - grp * PAIRS_PER_GROUP
    klo = jnp.where(p >= 5, 2 * one, jnp.where(p >= 3, one, 0 * one))
    khi = p - jnp.where(klo == 0, 0, jnp.where(klo == 1, 3, 5)) + klo + 1
    plan_ref[...] = jnp.zeros_like(plan_ref)
    plan_ref[PLAN_E_LO:PLAN_E_LO + 1, :] = grp * EXPERTS_PER_GROUP + klo
    plan_ref[PLAN_E_HI:PLAN_E_HI + 1, :] = grp * EXPERTS_PER_GROUP + khi
    plan_ref[PLAN_N_TILES:PLAN_N_TILES + 1, :] = jnp.broadcast_to(n_tiles, (1, LANES))


def _moe_plan(bucket):
    t = bucket.shape[1]
    return pl.pallas_call(
        functools.partial(_moe_plan_kernel, blk=512),
        out_shape=[jax.ShapeDtypeStruct((1, t), jnp.int32),
                   jax.ShapeDtypeStruct((PLAN_ROWS, LANES), jnp.int32)],
        compiler_params=pltpu.CompilerParams(vmem_limit_bytes=VMEM_LIMIT),
        name="moe_plan",
    )(bucket)


def _max_tiles(t):
    return (t + N_BUCKETS * (MOE_TILE - 1)) // MOE_TILE


def _row_copy(src, src_row, dst, dst_row, sem):
    return pltpu.make_async_copy(src.at[pl.ds(src_row, 1)], dst.at[pl.ds(dst_row, 1)], sem)


def _moe_scatter_kernel(pos_ref, x0_ref, x1_ref, tail_ref, buf_ref, out_ref, aug_ref, sem, *, n0):
    del buf_ref
    i = pl.program_id(0)
    tm, d = x0_ref.shape
    aug_ref[:, :d] = jnp.where(i < n0, x0_ref[...], x1_ref[...])
    aug_ref[:, d:] = tail_ref[...]

    for r in range(tm):
        _row_copy(aug_ref, r, out_ref, pos_ref[0, r], sem).start()
    pltpu.make_async_copy(aug_ref, out_ref.at[pl.ds(0, tm)], sem).wait()


def _sorted_buffer(t, d):
    return jnp.zeros((_max_tiles(t) * MOE_TILE, d + TAIL), F32)


def _moe_scatter(xs, streams, tail, pos3, tm, buf):
    d = xs[0].shape[1]
    t = tail.shape[0]
    n0 = streams[0].rows // tm
    return pl.pallas_call(
        functools.partial(_moe_scatter_kernel, n0=n0),
        grid=(t // tm,),
        in_specs=[pl.BlockSpec((None, 1, tm), lambda i: (i, 0, 0), memory_space=pltpu.SMEM),
                  *_two_stream_specs(streams, tm, d),
                  pl.BlockSpec((tm, TAIL), lambda i: (i, 0)),
                  pl.BlockSpec(memory_space=pl.ANY)],
        out_specs=pl.BlockSpec(memory_space=pl.ANY),
        out_shape=jax.ShapeDtypeStruct(buf.shape, F32),
        input_output_aliases={4: 0},
        scratch_shapes=[pltpu.VMEM((tm, d + TAIL), F32), pltpu.SemaphoreType.DMA(())],
        compiler_params=_params("arbitrary"),
        name="moe_scatter",
    )(pos3, *xs, tail, buf)


def _moe_expert_kernel(plan_ref, xa_ref, g_ref, mt_ref, wg0_ref, wu0_ref, wd0_ref, wg1_ref, wu1_ref, wd1_ref,
                       o_ref, wgu_ref, wd_ref):
    i = pl.program_id(0)
    n_tiles = plan_ref[PLAN_N_TILES * LANES]
    d = o_ref.shape[1]
    f = wg0_ref.shape[1]

    cur = jnp.minimum(i, n_tiles - 1)
    prev = jnp.minimum(jnp.maximum(i - 1, 0), n_tiles - 1)
    changed = ((i == 0) | (plan_ref[PLAN_E_LO * LANES + cur] != plan_ref[PLAN_E_LO * LANES + prev])
               | (plan_ref[PLAN_E_HI * LANES + cur] != plan_ref[PLAN_E_HI * LANES + prev]))

    @pl.when(changed)
    def _():
        for k, (wg_ref, wu_ref, wdn_ref) in enumerate(((wg0_ref, wu0_ref, wd0_ref), (wg1_ref, wu1_ref, wd1_ref))):
            wgu_ref[k, :, :f] = wg_ref[...].astype(BF16)
            wgu_ref[k, :, f:] = wu_ref[...].astype(BF16)
            wd_ref[k] = wdn_ref[...].astype(BF16)

    @pl.when(i < n_tiles)
    def _():
        x = xa_ref[:, :d]
        gate_lo = xa_ref[:, d:d + 1]
        gate_hi = xa_ref[:, d + 1:d + 2]
        mrow = xa_ref[:, d + 2:d + 3]

        def pick(kind):
            base = 3 * kind
            return jnp.where(mrow < 0.5, mt_ref[base:base + 1, :],
                             jnp.where(mrow < 1.5, mt_ref[base + 1:base + 2, :], mt_ref[base + 2:base + 3, :]))

        h = _adaln(x, g_ref[...], pick(0), pick(1)).astype(BF16)
        out = None
        for k, gate in enumerate((gate_lo, gate_hi)):
            gu = jnp.dot(h, wgu_ref[k], preferred_element_type=F32)
            act = (_silu(gu[:, :f]) * gu[:, f:] * gate).astype(BF16)
            y = jnp.dot(act, wd_ref[k], preferred_element_type=F32)
            out = y if out is None else out + y
        o_ref[...] = x + pick(2) * out

    @pl.when(i >= n_tiles)
    def _():
        o_ref[...] = jnp.zeros_like(o_ref)


def _moe_experts(xa, plan, g, mtab, layer, wg, wu, wd):
    rows, da = xa.shape
    d = da - TAIL
    f = wg.shape[3]
    nt = rows // MOE_TILE

    def tile(i, plan_ref):
        return jnp.minimum(i, plan_ref[PLAN_N_TILES * LANES] - 1)

    def wspec(shape, row):
        return pl.BlockSpec((None, None, *shape), lambda i, p: (layer, p[row * LANES + tile(i, p)], 0, 0))

    grid_spec = pltpu.PrefetchScalarGridSpec(
        num_scalar_prefetch=1,
        grid=(nt,),
        in_specs=[pl.BlockSpec((MOE_TILE, da), lambda i, p: (tile(i, p), 0)),
                  pl.BlockSpec((1, d), lambda i, p: (0, 0)),
                  pl.BlockSpec(mtab.shape, lambda i, p: (0, 0)),
                  wspec((d, f), PLAN_E_LO), wspec((d, f), PLAN_E_LO), wspec((f, d), PLAN_E_LO),
                  wspec((d, f), PLAN_E_HI), wspec((d, f), PLAN_E_HI), wspec((f, d), PLAN_E_HI)],
        out_specs=pl.BlockSpec((MOE_TILE, d), lambda i, p: (i, 0)),
        scratch_shapes=[pltpu.VMEM((2, d, 2 * f), BF16), pltpu.VMEM((2, f, d), BF16)])
    return pl.pallas_call(
        _moe_expert_kernel,
        grid_spec=grid_spec,
        out_shape=jax.ShapeDtypeStruct((rows, d), F32),
        compiler_params=_params("arbitrary"),
        name="moe_experts",
    )(plan.reshape(-1), xa, g, mtab, wg, wu, wd, wg, wu, wd)


def _moe_gather_kernel(pos_ref, ys_ref, o_ref, sem):
    tm = o_ref.shape[0]

    for r in range(tm):
        _row_copy(ys_ref, pos_ref[0, r], o_ref, r, sem).start()
    pltpu.make_async_copy(ys_ref.at[pl.ds(0, tm)], o_ref, sem).wait()


def _moe_gather(ys, pos3, tile0, rows, tm):
    d = ys.shape[1]
    return pl.pallas_call(
        _moe_gather_kernel,
        grid=(rows // tm,),
        in_specs=[pl.BlockSpec((None, 1, tm), lambda i: (i + tile0, 0, 0), memory_space=pltpu.SMEM),
                  pl.BlockSpec(memory_space=pl.ANY)],
        out_specs=pl.BlockSpec((tm, d), lambda i: (i, 0)),
        out_shape=jax.ShapeDtypeStruct((rows, d), F32),
        scratch_shapes=[pltpu.SemaphoreType.DMA(())],
        compiler_params=_params("arbitrary"),
        name="moe_gather",
    )(pos3, ys)


def _moe(xs, streams, g, mod, layer, rw_t, rb_col, wg, wu, wd, buf, tm=512):
    tail, bucket = _moe_route(xs, streams, g, mod, layer, rw_t, rb_col)
    pos, plan = _moe_plan(bucket)
    pos3 = pos.reshape(-1, 1, tm)
    xa = _moe_scatter(xs, streams, tail, pos3, tm, buf)
    mtab = mod[layer, 3:6, :, 0, :].reshape(-1, xs[0].shape[1])
    ys = _moe_experts(xa, plan, g, mtab, layer, wg, wu, wd)
    out, tile0 = [], 0
    for st in streams:
        out.append(_moe_gather(ys, pos3, tile0, st.rows, tm))
        tile0 += st.rows // tm
    return out, xa


def _rope_tables(t_len):
    rows = t_len // GRID_W
    row_ids = jnp.repeat(jnp.arange(rows), GRID_W).astype(F32)
    col_ids = jnp.tile(jnp.arange(GRID_W), rows).astype(F32)
    n_freq = HEAD_DIM // 4
    inv = 1.0 / (ROPE_THETA ** (jnp.arange(n_freq, dtype=F32) / n_freq))
    ang = jnp.concatenate([row_ids[:, None] * inv, col_ids[:, None] * inv], axis=-1)
    cos, sin = jnp.cos(ang), jnp.sin(ang)
    cs = jnp.tile(jnp.concatenate([cos, cos], axis=-1), (1, LANES // HEAD_DIM))
    sn = jnp.tile(jnp.concatenate([-sin, sin], axis=-1), (1, LANES // HEAD_DIM))
    return cs, sn


def _pad_lanes(v):
    return jnp.pad(v.reshape(1, -1), ((0, 0), (0, LANES - v.size)))


def kernel(x_prompt, x_sample, c, c_ctx, cache_k, cache_v, state_ssm, mod_w, mod_b, norm1_g, norm2_g,
           attn_w_qkv, attn_q_norm_g, attn_k_norm_g, attn_w_o, ssd_in_proj, ssd_conv_w, ssd_conv_b,
           ssd_dt_bias, ssd_a_log, ssd_d, ssd_norm_g, ssd_out_proj, router_w, router_b, expert_w_gate,
           expert_w_up, expert_w_down):
    n_p, l_p, d = x_prompt.shape
    n_s, l_s, _ = x_sample.shape
    depth = mod_w.shape[0]
    past = cache_k.shape[2]
    nkv = N_KV_HEADS * HEAD_DIM
    assert d == D_MODEL and 1 + n_s == 3
    streams = (_Stream(n_p, l_p, 0, 0), _Stream(n_s, l_s, 1, 1))
    xs = [x_prompt.reshape(n_p * l_p, d), x_sample.reshape(n_s * l_s, d)]

    cond8 = jnp.concatenate([c_ctx[None], c, jnp.zeros((8 - 1 - n_s, d), F32)], axis=0)
    mod = _modulation(cond8, mod_w, mod_b)
    mod = mod[:, :1 + n_s].reshape(depth, 1 + n_s, 6, 1, d).transpose(0, 2, 1, 3, 4)

    rope = _rope_tables(l_s)
    rw_t = jnp.pad(router_w.T, ((0, LANES - N_EXPERTS), (0, 0))).astype(BF16)
    rb_col = jnp.pad(router_b, (0, LANES - N_EXPERTS)).reshape(LANES, 1)

    sorted_buf = _sorted_buffer(sum(st.rows for st in streams), d)
    new_k, new_v, new_ssm = [], [], []
    for layer in range(depth):
        i = layer // 2
        g1 = norm1_g[layer][None]
        if layer % 2 == 0:
            w_qkv = attn_w_qkv[i].astype(BF16)
            w_o = attn_w_o[i].astype(BF16)
            qg2 = jnp.tile(attn_q_norm_g[i], LANES // HEAD_DIM)[None]
            kg2 = jnp.tile(attn_k_norm_g[i], LANES // HEAD_DIM)[None]
            for si, st in enumerate(streams):
                if si == 0:
                    q, k2, v2, kc, vc = _qk_prep(xs[si], g1, mod, layer, st, w_qkv, qg2, kg2, emit_cache=True)
                    o = _attention(q, k2, v2, st.n_seq, st.seq_len)
                    new_k.append(kc.reshape(n_p, l_p, N_KV_HEADS, HEAD_DIM))
                    new_v.append(vc.reshape(n_p, l_p, N_KV_HEADS, HEAD_DIM))
                else:
                    q, k2, v2 = _qk_prep(xs[si], g1, mod, layer, st, w_qkv, qg2, kg2, tables=rope)
                    kc2, vc2 = _kv_expand(cache_k[:, i].reshape(n_s * past, nkv),
                                          cache_v[:, i].reshape(n_s * past, nkv))
                    o = _attention(q, k2, v2, st.n_seq, st.seq_len, ctx=(kc2, vc2, past))
                xs[si] = _mm_res(o, w_o, xs[si], mod, layer, st)
        else:
            w_in = ssd_in_proj[i]
            nzx = D_INNER + CONV_DIM
            w_zx = w_in[:, :nzx].astype(BF16)
            w_dt = jnp.pad(w_in[:, nzx:], ((0, 0), (0, LANES - 2 * SSD_HEADS))).astype(BF16)
            w_out = ssd_out_proj[i].astype(BF16)
            dtb = _pad_lanes(ssd_dt_bias[i])
            alog = _pad_lanes(ssd_a_log[i])
            dskip = jnp.repeat(ssd_d[i], SSD_HEAD_DIM)[None]
            cw, cb = ssd_conv_w[i], ssd_conv_b[i][None]
            for si, st in enumerate(streams):
                z, xbc, dt = _in_proj_conv(xs[si], g1, mod, layer, st, w_zx, w_dt, cw, cb)
                init = None if si == 0 else state_ssm[:, i].reshape(n_s, 2, D_INNER, D_STATE)
                y_f, s_f = _ssd_scan(xbc, dt, dtb, alog, st, rev=False, init=init)
                yg, s_b = _ssd_scan(xbc, dt, dtb, alog, st, rev=True, init=init, final=(y_f, z, dskip))
                if si == 0:
                    new_ssm.append(jnp.stack([s_f, s_b], axis=1).reshape(
                        n_p, 2, SSD_HEADS, SSD_HEAD_DIM, D_STATE))
                xs[si] = _mm_res(yg, w_out, xs[si], mod, layer, st, norm_g=ssd_norm_g[i][None])
        xs, sorted_buf = _moe(xs, streams, norm2_g[layer][None], mod, layer, rw_t, rb_col,
                              expert_w_gate, expert_w_up, expert_w_down, sorted_buf)

    return (xs[0].reshape(n_p, l_p, d), xs[1].reshape(n_s, l_s, d),
            jnp.stack(new_k, axis=1), jnp.stack(new_v, axis=1), jnp.stack(new_ssm, axis=1))
```

```python
import functools
import math
from typing import NamedTuple

import jax
import jax.numpy as jnp
from jax import lax
from jax.experimental import pallas as pl
from jax.experimental.pallas import tpu as pltpu

F32 = jnp.float32
BF16 = jnp.bfloat16

D_MODEL = 1024
GRID_W = 64
N_HEADS = 16
N_KV_HEADS = 4
HEAD_DIM = 64
KV_REP = N_HEADS // N_KV_HEADS
ROPE_THETA = 10000.0
D_INNER = 2048
SSD_HEAD_DIM = 64
SSD_HEADS = 32
SSD_GROUPS = 4
SSD_REP = SSD_HEADS // SSD_GROUPS
D_STATE = 128
D_CONV = 5
CHUNK = 128
CONV_DIM = D_INNER + 2 * SSD_GROUPS * D_STATE
N_EXPERTS = 16
N_EXPERT_GROUPS = 4
EXPERTS_PER_GROUP = 4
D_FF_EXPERT = 256
EPS = 1e-6

LANES = 128
VMEM_LIMIT = 56 * 1024 * 1024


def _params(*sem):
    return pltpu.CompilerParams(dimension_semantics=sem, vmem_limit_bytes=VMEM_LIMIT)


def _silu(x):
    return x * jax.nn.sigmoid(x)


class _Stream(NamedTuple):
    n_seq: int
    seq_len: int
    mod_base: int
    mod_step: int

    @property
    def rows(self):
        return self.n_seq * self.seq_len


def _mod_kernel(c_ref, w_ref, b_ref, o_ref):
    s = _silu(c_ref[...])
    o_ref[...] = jnp.dot(s.astype(BF16), w_ref[...].astype(BF16),
                         preferred_element_type=F32) + b_ref[...]


def _modulation(cond8, mod_w, mod_b):
    depth, d, n = mod_w.shape
    tn = 1536
    return pl.pallas_call(
        _mod_kernel,
        grid=(depth, n // tn),
        in_specs=[pl.BlockSpec((8, d), lambda l, j: (0, 0)),
                  pl.BlockSpec((None, d, tn), lambda l, j: (l, 0, j)),
                  pl.BlockSpec((None, 1, tn), lambda l, j: (l, 0, j))],
        out_specs=pl.BlockSpec((None, 8, tn), lambda l, j: (l, 0, j)),
        out_shape=jax.ShapeDtypeStruct((depth, 8, n), F32),
        compiler_params=_params("parallel", "parallel"),
        name="modulation",
    )(cond8, mod_w, mod_b.reshape(depth, 1, n))


def _mod_spec(layer, chunk, tm, st):
    def imap(i, *_):
        return (layer, chunk, st.mod_base + st.mod_step * (i * tm // st.seq_len), 0, 0)
    return pl.BlockSpec((None, None, None, 1, D_MODEL), imap)


def _adaln(x, g, sh, sc):
    ms = jnp.mean(x * x, axis=-1, keepdims=True)
    y = x * lax.rsqrt(ms + EPS) * g
    return y * (1 + sc) + sh


def _mm_res_kernel(a_ref, w_ref, r_ref, gate_ref, o_ref):
    y = jnp.dot(a_ref[...], w_ref[...], preferred_element_type=F32)
    o_ref[...] = r_ref[...] + gate_ref[...] * y


def _norm_mm_res_kernel(a_ref, ng_ref, w_ref, r_ref, gate_ref, o_ref):
    a = a_ref[...]
    ms = jnp.mean(a * a, axis=-1, keepdims=True)
    an = (a * lax.rsqrt(ms + EPS) * ng_ref[...]).astype(BF16)
    y = jnp.dot(an, w_ref[...], preferred_element_type=F32)
    o_ref[...] = r_ref[...] + gate_ref[...] * y


def _mm_res(a, w, res, mod, layer, st, norm_g=None, tm=512):
    t, k = a.shape
    n = w.shape[1]
    a_spec = pl.BlockSpec((tm, k), lambda i: (i, 0))
    w_spec = pl.BlockSpec((k, n), lambda i: (0, 0))
    r_spec = pl.BlockSpec((tm, n), lambda i: (i, 0))
    gate_spec = _mod_spec(layer, 2, tm, st)
    if norm_g is None:
        kern, specs, args = _mm_res_kernel, [a_spec, w_spec, r_spec, gate_spec], (a, w, res, mod)
    else:
        kern = _norm_mm_res_kernel
        specs = [a_spec, pl.BlockSpec((1, k), lambda i: (0, 0)), w_spec, r_spec, gate_spec]
        args = (a, norm_g, w, res, mod)
    return pl.pallas_call(
        kern,
        grid=(t // tm,),
        in_specs=specs,
        out_specs=pl.BlockSpec((tm, n), lambda i: (i, 0)),
        out_shape=jax.ShapeDtypeStruct((t, n), F32),
        compiler_params=_params("parallel"),
        name="mm_res",
    )(*args)


def _lane_iota(rows):
    return lax.broadcasted_iota(jnp.int32, (rows, LANES), 1)


def _head_pair_norm(c, g2):
    lo = _lane_iota(c.shape[0]) < HEAD_DIM
    cc = c * c
    s_lo = jnp.sum(jnp.where(lo, cc, 0.0), axis=-1, keepdims=True)
    s_hi = jnp.sum(jnp.where(lo, 0.0, cc), axis=-1, keepdims=True)
    r = jnp.where(lo, lax.rsqrt(s_lo / HEAD_DIM + EPS), lax.rsqrt(s_hi / HEAD_DIM + EPS))
    return c * r * g2


def _rope_pair(y, cs, sn):
    first = (_lane_iota(y.shape[0]) % HEAD_DIM) < HEAD_DIM // 2
    partner = jnp.where(first, pltpu.roll(y, LANES - HEAD_DIM // 2, 1), pltpu.roll(y, HEAD_DIM // 2, 1))
    return y * cs + partner * sn


def _store_v_split(ref, j, c):
    lane = _lane_iota(c.shape[0])
    lo = lane < HEAD_DIM
    a_lo = jnp.where(lo, c, jnp.where(lane == HEAD_DIM, 1.0, 0.0))
    b_hi = jnp.where(lo, jnp.where(lane == 0, 1.0, 0.0), c)
    ref[2 * j, 0] = a_lo.astype(BF16)
    ref[2 * j, 1] = pltpu.roll(a_lo, HEAD_DIM, 1).astype(BF16)
    ref[2 * j + 1, 0] = pltpu.roll(b_hi, HEAD_DIM, 1).astype(BF16)
    ref[2 * j + 1, 1] = b_hi.astype(BF16)


def _store_kt_split(ref, j, c):
    t = c.T
    a, b = t[:HEAD_DIM], t[HEAD_DIM:]
    z = jnp.zeros_like(a)
    ref[2 * j, 0] = jnp.concatenate([a, z], axis=0).astype(BF16)
    ref[2 * j, 1] = jnp.concatenate([z, a], axis=0).astype(BF16)
    ref[2 * j + 1, 0] = jnp.concatenate([b, z], axis=0).astype(BF16)
    ref[2 * j + 1, 1] = jnp.concatenate([z, b], axis=0).astype(BF16)


def _qk_prep_kernel(*refs, rope, emit_cache):
    it = iter(refs)
    qkv_ref, qg_ref, kg_ref = next(it), next(it), next(it)
    cs_ref, sn_ref = (next(it), next(it)) if rope else (None, None)
    q_ref, k2_ref, v2_ref = next(it), next(it), next(it)
    kc_ref, vc_ref = (next(it), next(it)) if emit_cache else (None, None)

    nq = N_HEADS * HEAD_DIM
    nkv = N_KV_HEADS * HEAD_DIM
    scale = HEAD_DIM ** -0.5 * math.log2(math.e)
    for j in range(nq // LANES):
        y = _head_pair_norm(qkv_ref[:, j * LANES:(j + 1) * LANES], qg_ref[...])
        if rope:
            y = _rope_pair(y, cs_ref[...], sn_ref[...])
        q_ref[:, j * LANES:(j + 1) * LANES] = (y * scale).astype(BF16)
    for j in range(nkv // LANES):
        y = _head_pair_norm(qkv_ref[:, nq + j * LANES:nq + (j + 1) * LANES], kg_ref[...])
        if rope:
            y = _rope_pair(y, cs_ref[...], sn_ref[...])
        if emit_cache:
            kc_ref[:, j * LANES:(j + 1) * LANES] = y
        _store_kt_split(k2_ref, j, y)
        v = qkv_ref[:, nq + nkv + j * LANES:nq + nkv + (j + 1) * LANES]
        if emit_cache:
            vc_ref[:, j * LANES:(j + 1) * LANES] = v
        _store_v_split(v2_ref, j, v)


def _kv_shapes(rows):
    return (jax.ShapeDtypeStruct((N_KV_HEADS, 2, LANES, rows), BF16),
            jax.ShapeDtypeStruct((N_KV_HEADS, 2, rows, LANES), BF16))


def _kv_specs(tm):
    return (pl.BlockSpec((N_KV_HEADS, 2, LANES, tm), lambda i: (0, 0, 0, i)),
            pl.BlockSpec((N_KV_HEADS, 2, tm, LANES), lambda i: (0, 0, i, 0)))


def _qkv_kernel(x_ref, g_ref, sh_ref, sc_ref, w_ref, *refs, rope, emit_cache):
    qkv_ref = refs[-1]
    h = _adaln(x_ref[...], g_ref[...], sh_ref[...], sc_ref[...]).astype(BF16)
    qkv_ref[...] = jnp.dot(h, w_ref[...], preferred_element_type=F32)
    _qk_prep_kernel(qkv_ref, *refs[:-1], rope=rope, emit_cache=emit_cache)


def _qk_prep(x, g, mod, layer, st, w, qg2, kg2, tables=None, emit_cache=False, tm=512):
    rows, d = x.shape
    n = w.shape[1]
    nq = N_HEADS * HEAD_DIM
    nkv = N_KV_HEADS * HEAD_DIM
    rope = tables is not None
    in_specs = [pl.BlockSpec((tm, d), lambda i: (i, 0)),
                pl.BlockSpec((1, d), lambda i: (0, 0)),
                _mod_spec(layer, 0, tm, st),
                _mod_spec(layer, 1, tm, st),
                pl.BlockSpec((d, n), lambda i: (0, 0)),
                pl.BlockSpec((1, LANES), lambda i: (0, 0)),
                pl.BlockSpec((1, LANES), lambda i: (0, 0))]
    args = [x, g, mod, mod, w, qg2, kg2]
    if rope:
        nt = tables[0].shape[0] // tm
        in_specs += [pl.BlockSpec((tm, LANES), lambda i: (i % nt, 0))] * 2
        args += list(tables)
    out_shape = [jax.ShapeDtypeStruct((rows, nq), BF16), *_kv_shapes(rows)]
    out_specs = [pl.BlockSpec((tm, nq), lambda i: (i, 0)), *_kv_specs(tm)]
    if emit_cache:
        out_shape += [jax.ShapeDtypeStruct((rows, nkv), F32)] * 2
        out_specs += [pl.BlockSpec((tm, nkv), lambda i: (i, 0))] * 2
    return pl.pallas_call(
        functools.partial(_qkv_kernel, rope=rope, emit_cache=emit_cache),
        grid=(rows // tm,),
        in_specs=in_specs,
        out_specs=out_specs,
        out_shape=out_shape,
        scratch_shapes=[pltpu.VMEM((tm, n), F32)],
        compiler_params=_params("parallel"),
        name="qkv_prep",
    )(*args)


def _kv_expand_kernel(k_ref, v_ref, k2_ref, v2_ref):
    for j in range(N_KV_HEADS * HEAD_DIM // LANES):
        _store_kt_split(k2_ref, j, k_ref[:, j * LANES:(j + 1) * LANES])
        _store_v_split(v2_ref, j, v_ref[:, j * LANES:(j + 1) * LANES])


def _kv_expand(k, v, tm=512):
    rows, nkv = k.shape
    return pl.pallas_call(
        _kv_expand_kernel,
        grid=(rows // tm,),
        in_specs=[pl.BlockSpec((tm, nkv), lambda i: (i, 0))] * 2,
        out_specs=list(_kv_specs(tm)),
        out_shape=list(_kv_shapes(rows)),
        compiler_params=_params("parallel"),
        name="kv_expand",
    )(k, v)


def _attn_kernel(*refs, tk, n_tiles, has_ctx):
    if has_ctx:
        q_ref, k_ref, v_ref, kc_ref, vc_ref, o_ref = refs
    else:
        q_ref, k_ref, v_ref, o_ref = refs
        kc_ref = vc_ref = None
    tq = q_ref.shape[0]
    lo = _lane_iota(tq) < HEAD_DIM
    tiles = [(k_ref, v_ref, t) for t in range(n_tiles)]
    if has_ctx:
        tiles += [(kc_ref, vc_ref, t) for t in range(vc_ref.shape[2] // tk)]
    for gi in range(k_ref.shape[0]):
        for pair in range(KV_REP // 2):
            c0 = (gi * (KV_REP // 2) + pair) * LANES
            qp = q_ref[:, c0:c0 + LANES]
            m = [jnp.full((tq, 1), -jnp.inf, F32)] * 2
            acc = [jnp.zeros((tq, LANES), F32)] * 2
            for kr, vr, t in tiles:
                for half in range(2):
                    s = jnp.dot(qp, kr[gi, half, :, t * tk:(t + 1) * tk], preferred_element_type=F32)
                    m_new = jnp.maximum(m[half], jnp.max(s, axis=-1, keepdims=True))
                    p = jnp.exp2(s - m_new).astype(BF16)
                    acc[half] = jnp.exp2(m[half] - m_new) * acc[half] + jnp.dot(
                        p, vr[gi, half, t * tk:(t + 1) * tk, :], preferred_element_type=F32)
                    m[half] = m_new
            out_lo = acc[0] / acc[0][:, HEAD_DIM:HEAD_DIM + 1]
            out_hi = acc[1] / acc[1][:, 0:1]
            o_ref[:, c0:c0 + LANES] = jnp.where(lo, out_lo, out_hi).astype(BF16)


def _attention(q, k2, v2, n_batch, t_len, ctx=None, tq=1024, tk=512, groups=1):
    tk = min(tk, t_len)
    tq = min(tq, t_len)
    nq = t_len // tq
    gw = groups * KV_REP * HEAD_DIM
    q_spec = pl.BlockSpec((tq, gw), lambda b, g, i: (b * nq + i, g))
    in_specs = [q_spec,
                pl.BlockSpec((groups, 2, LANES, t_len), lambda b, g, i: (g, 0, 0, b)),
                pl.BlockSpec((groups, 2, t_len, LANES), lambda b, g, i: (g, 0, b, 0))]
    args = [q, k2, v2]
    if ctx is not None:
        kc2, vc2, ctx_len = ctx
        in_specs += [pl.BlockSpec((groups, 2, LANES, ctx_len), lambda b, g, i: (g, 0, 0, b)),
                     pl.BlockSpec((groups, 2, ctx_len, LANES), lambda b, g, i: (g, 0, b, 0))]
        args += [kc2, vc2]
    return pl.pallas_call(
        functools.partial(_attn_kernel, tk=tk, n_tiles=t_len // tk, has_ctx=ctx is not None),
        grid=(n_batch, N_KV_HEADS // groups, nq),
        in_specs=in_specs,
        out_specs=q_spec,
        out_shape=jax.ShapeDtypeStruct(q.shape, BF16),
        compiler_params=_params("parallel", "parallel", "arbitrary"),
        name="attention",
    )(*args)


HALO = 16


def _in_proj_kernel(x_ref, xa_ref, xb_ref, g_ref, sh_ref, sc_ref, w_ref, wdt_ref, cw_ref, cb_ref,
                    z_ref, xbc_ref, dt_ref, h_ref, u_ref, *, n_z_tiles, seq_len):
    i, j = pl.program_id(0), pl.program_id(1)
    tm = x_ref.shape[0]

    @pl.when(j == 0)
    def _():
        def norm(ref):
            return _adaln(ref[...], g_ref[...], sh_ref[...], sc_ref[...]).astype(BF16)
        h_ref[0:HALO] = norm(xa_ref)
        h = norm(x_ref)
        h_ref[HALO:HALO + tm] = h
        h_ref[HALO + tm:] = norm(xb_ref)
        dt_ref[...] = jnp.dot(h, wdt_ref[...], preferred_element_type=F32)

    @pl.when(j < n_z_tiles)
    def _():
        z_ref[...] = jnp.dot(h_ref[HALO:HALO + tm], w_ref[...], preferred_element_type=F32)

    @pl.when(j >= n_z_tiles)
    def _():
        u = jnp.dot(h_ref[...], w_ref[...], preferred_element_type=F32)
        keep_a = ((i * tm) % seq_len != 0).astype(F32)
        keep_b = (((i + 1) * tm) % seq_len != 0).astype(F32)
        u_ref[0:HALO] = u[0:HALO] * keep_a
        u_ref[HALO:HALO + tm] = u[HALO:HALO + tm]
        u_ref[HALO + tm:] = u[HALO + tm:] * keep_b
        acc = cb_ref[...]
        for k in range(D_CONV):
            d = k - D_CONV // 2
            acc = acc + u_ref[HALO + d:HALO + d + tm, :] * cw_ref[k:k + 1, :]
        xbc_ref[...] = _silu(acc)


def _in_proj_conv(x, g, mod, layer, st, w_zx, w_dt, conv_w, conv_b, tm=512, tn=1024):
    t, d = x.shape
    tm = min(tm, st.seq_len)
    assert st.seq_len % tm == 0 and D_INNER % tn == 0 and CONV_DIM % tn == 0
    nzt, nct = D_INNER // tn, CONV_DIM // tn
    hb = tm // HALO
    last_halo = t // HALO - 1
    return pl.pallas_call(
        functools.partial(_in_proj_kernel, n_z_tiles=nzt, seq_len=st.seq_len),
        grid=(t // tm, nzt + nct),
        in_specs=[pl.BlockSpec((tm, d), lambda i, j: (i, 0)),
                  pl.BlockSpec((HALO, d), lambda i, j: (jnp.maximum(i * hb - 1, 0), 0)),
                  pl.BlockSpec((HALO, d), lambda i, j: (jnp.minimum((i + 1) * hb, last_halo), 0)),
                  pl.BlockSpec((1, d), lambda i, j: (0, 0)),
                  _mod_spec(layer, 0, tm, st),
                  _mod_spec(layer, 1, tm, st),
                  pl.BlockSpec((d, tn), lambda i, j: (0, j)),
                  pl.BlockSpec((d, LANES), lambda i, j: (0, 0)),
                  pl.BlockSpec((D_CONV, tn), lambda i, j: (0, jnp.maximum(j - nzt, 0))),
                  pl.BlockSpec((1, tn), lambda i, j: (0, jnp.maximum(j - nzt, 0)))],
        out_specs=[pl.BlockSpec((tm, tn), lambda i, j: (i, jnp.minimum(j, nzt - 1))),
                   pl.BlockSpec((tm, tn), lambda i, j: (i, jnp.maximum(j - nzt, 0))),
                   pl.BlockSpec((tm, LANES), lambda i, j: (i, 0))],
        out_shape=[jax.ShapeDtypeStruct((t, D_INNER), F32),
                   jax.ShapeDtypeStruct((t, CONV_DIM), F32),
                   jax.ShapeDtypeStruct((t, LANES), F32)],
        scratch_shapes=[pltpu.VMEM((tm + 2 * HALO, d), BF16), pltpu.VMEM((tm + 2 * HALO, tn), F32)],
        compiler_params=_params("parallel", "arbitrary"),
        name="in_proj_conv",
    )(x, x, x, g, mod, mod, w_zx, w_dt, conv_w, conv_b)


def _cumsum_rows(v, reverse):
    n = v.shape[0]
    row = lax.broadcasted_iota(jnp.int32, v.shape, 0)
    k = 1
    while k < n:
        if reverse:
            v = v + jnp.where(row < n - k, pltpu.roll(v, n - k, 0), 0.0)
        else:
            v = v + jnp.where(row >= k, pltpu.roll(v, k, 0), 0.0)
        k *= 2
    return v


def _ssd_kernel(*refs, rev, final_pass, has_init, cps):
    it = iter(refs)
    x_ref, b_ref, c_ref, dt_ref, dtb_ref, alog_ref = [next(it) for _ in range(6)]
    init_ref = next(it) if has_init else None
    if final_pass:
        y0_ref, z_ref, dskip_ref = next(it), next(it), next(it)
    y_ref, fin_ref, st_ref = next(it), next(it), next(it)

    ci = pl.program_id(1)
    n_blk = D_INNER // LANES

    @pl.when(ci == 0)
    def _():
        if has_init:
            for k in range(n_blk):
                st_ref[:, k * LANES:(k + 1) * LANES] = init_ref[k * LANES:(k + 1) * LANES, :].T
        else:
            st_ref[...] = jnp.zeros_like(st_ref)

    dirn = 1 if rev else 0
    neg_a = -jnp.exp(alog_ref[...])
    row = lax.broadcasted_iota(jnp.int32, (CHUNK, CHUNK), 0)
    col = lax.broadcasted_iota(jnp.int32, (CHUNK, CHUNK), 1)
    causal = (col >= row) if rev else (col <= row)
    lo = _lane_iota(CHUNK) < SSD_HEAD_DIM
    last = 0 if rev else CHUNK - 1

    for cc in (reversed(range(cps)) if rev else range(cps)):
        rows = slice(cc * CHUNK, (cc + 1) * CHUNK)
        dtv = jax.nn.softplus(dt_ref[rows, :] + dtb_ref[...])
        acum = _cumsum_rows(dtv * neg_a, rev) * math.log2(math.e)
        acum_t = acum.T
        dtv_t = dtv.T
        w_t = dtv_t * jnp.exp2(acum_t[:, last:last + 1] - acum_t)
        e_tot = jnp.exp2(acum[last:last + 1, :])
        src_t = acum_t - jnp.log2(dtv_t)
        for g in range(SSD_GROUPS):
            bg = b_ref[rows, g * D_STATE:(g + 1) * D_STATE]
            cg = c_ref[rows, g * D_STATE:(g + 1) * D_STATE]
            cb = lax.dot_general(cg.astype(BF16), bg.astype(BF16), (((1,), (1,)), ((), ())),
                                 preferred_element_type=F32)
            bg_t = bg.T
            for pp in range(SSD_REP // 2):
                h0 = g * SSD_REP + 2 * pp
                i0 = dirn * SSD_HEADS + h0
                cols = slice(h0 * SSD_HEAD_DIM, (h0 + 2) * SSD_HEAD_DIM)
                xp = x_ref[rows, cols]
                st_pair = st_ref[:, cols]
                halves = ((jnp.where(lo, xp, 0.0).astype(BF16), jnp.where(lo, st_pair, 0.0).astype(BF16)),
                          (jnp.where(lo, 0.0, xp).astype(BF16), jnp.where(lo, 0.0, st_pair).astype(BF16)))
                y = new = None
                for hh, (xh, sth) in enumerate(halves):
                    i = i0 + hh
                    a_col = jnp.broadcast_to(acum[:, i:i + 1], (CHUNK, CHUNK))
                    lmat = jnp.exp2(jnp.where(causal, a_col - src_t[i:i + 1, :], -jnp.inf))
                    m = (cb * lmat).astype(BF16)
                    ce = (cg * jnp.exp2(a_col)).astype(BF16)
                    yh = jnp.dot(jnp.concatenate([m, ce], axis=1), jnp.concatenate([xh, sth], axis=0),
                                 preferred_element_type=F32)
                    nh = jnp.dot((bg_t * w_t[i:i + 1, :]).astype(BF16), xh, preferred_element_type=F32)
                    y = yh if y is None else y + yh
                    new = nh if new is None else new + nh
                dec = jnp.where(lo[:1], e_tot[:, i0:i0 + 1], e_tot[:, i0 + 1:i0 + 2])
                st_ref[:, cols] = st_pair * dec + new
                if final_pass:
                    y = (y0_ref[rows, cols] + y + xp * dskip_ref[:, cols]) * _silu(z_ref[rows, cols])
                y_ref[rows, cols] = y

    @pl.when(ci == pl.num_programs(1) - 1)
    def _():
        for k in range(n_blk):
            fin_ref[k * LANES:(k + 1) * LANES, :] = st_ref[:, k * LANES:(k + 1) * LANES].T


def _ssd_scan(xbc, dt, dtb, alog, st, rev, init=None, final=None, cps=4):
    cps = min(cps, st.seq_len // CHUNK)
    blk = cps * CHUNK
    nb = st.seq_len // blk
    dirn = 1 if rev else 0

    def tok(s, c):
        return (s * nb + (nb - 1 - c if rev else c), 0)

    bc_w = SSD_GROUPS * D_STATE
    b_col = D_INNER // bc_w
    tok_spec = pl.BlockSpec((blk, D_INNER), tok)
    in_specs = [tok_spec,
                pl.BlockSpec((blk, bc_w), lambda s, c: (tok(s, c)[0], b_col)),
                pl.BlockSpec((blk, bc_w), lambda s, c: (tok(s, c)[0], b_col + 1)),
                pl.BlockSpec((blk, LANES), tok),
                pl.BlockSpec((1, LANES), lambda s, c: (0, 0)),
                pl.BlockSpec((1, LANES), lambda s, c: (0, 0))]
    args = [xbc, xbc, xbc, dt, dtb, alog]
    if init is not None:
        in_specs.append(pl.BlockSpec((None, None, D_INNER, D_STATE), lambda s, c: (s, dirn, 0, 0)))
        args.append(init)
    if final is not None:
        y0, zx, dskip = final
        in_specs += [tok_spec, tok_spec, pl.BlockSpec((1, D_INNER), lambda s, c: (0, 0))]
        args += [y0, zx, dskip]
    return pl.pallas_call(
        functools.partial(_ssd_kernel, rev=rev, final_pass=final is not None,
                          has_init=init is not None, cps=cps),
        grid=(st.n_seq, nb),
        in_specs=in_specs,
        out_specs=[tok_spec, pl.BlockSpec((None, D_INNER, D_STATE), lambda s, c: (s, 0, 0))],
        out_shape=[jax.ShapeDtypeStruct((st.rows, D_INNER), F32),
                   jax.ShapeDtypeStruct((st.n_seq, D_INNER, D_STATE), F32)],
        scratch_shapes=[pltpu.VMEM((D_STATE, D_INNER), F32)],
        compiler_params=_params("parallel", "arbitrary"),
        name="ssd_scan",
    )(*args)


MOE_TILE = 512
PAIRS_PER_GROUP = EXPERTS_PER_GROUP * (EXPERTS_PER_GROUP - 1) // 2
N_BUCKETS = N_EXPERT_GROUPS * PAIRS_PER_GROUP
BUCKET_ROWS = 32
TAIL = LANES


def _route_picks(biased_t, scores_t):
    ng, ne = N_EXPERT_GROUPS, EXPERTS_PER_GROUP
    b = [[biased_t[j * ne + k:j * ne + k + 1, :] for k in range(ne)] for j in range(ng)]
    s = [[scores_t[j * ne + k:j * ne + k + 1, :] for k in range(ne)] for j in range(ng)]
    gs = []
    for j in range(ng):
        best = None
        for k1 in range(ne):
            for k2 in range(k1 + 1, ne):
                ps = b[j][k1] + b[j][k2]
                best = ps if best is None else jnp.maximum(best, ps)
        gs.append(best)
    sel = jnp.zeros_like(gs[0], jnp.int32)
    best = gs[0]
    for j in range(1, ng):
        gt = gs[j] > best
        best = jnp.where(gt, gs[j], best)
        sel = jnp.where(gt, j, sel)
    in_grp = [sel == j for j in range(ng)]

    def pick(vals):
        out = []
        for k in range(ne):
            v = vals[ng - 1][k]
            for j in range(ng - 2, -1, -1):
                v = jnp.where(in_grp[j], vals[j][k], v)
            out.append(v)
        return out

    yb, ys = pick(b), pick(s)

    def argmax_first(vals):
        bv, bi = vals[0], jnp.zeros_like(sel)
        for k in range(1, ne):
            gt = vals[k] > bv
            bv = jnp.where(gt, vals[k], bv)
            bi = jnp.where(gt, k, bi)
        return bi

    i1 = argmax_first(yb)
    i2 = argmax_first([jnp.where(i1 == k, -jnp.inf, yb[k]) for k in range(ne)])

    def take(vals, idx):
        v = vals[ne - 1]
        for k in range(ne - 2, -1, -1):
            v = jnp.where(idx == k, vals[k], v)
        return v

    w1, w2 = take(ys, i1), take(ys, i2)
    tot = w1 + w2
    return sel, i1, i2, w1 / tot, w2 / tot


def _two_stream_specs(streams, tm, width):
    n0 = streams[0].rows // tm
    return (pl.BlockSpec((tm, width), lambda i, *_: (jnp.minimum(i, n0 - 1), 0)),
            pl.BlockSpec((tm, width), lambda i, *_: (jnp.maximum(i - n0, 0), 0)))


def _moe_route_kernel(x0_ref, x1_ref, g_ref, sh0_ref, sc0_ref, sh1_ref, sc1_ref, rw_ref, rb_ref,
                      tail_ref, bucket_ref, tt_ref, *, n0, mod_of_tile):
    i = pl.program_id(0)
    first = i < n0
    x = jnp.where(first, x0_ref[...], x1_ref[...])
    sh = jnp.where(first, sh0_ref[...], sh1_ref[...])
    sc = jnp.where(first, sc0_ref[...], sc1_ref[...])
    h = _adaln(x, g_ref[...], sh, sc).astype(BF16)
    logits_t = lax.dot_general(rw_ref[...], h, (((1,), (1,)), ((), ())),
                               preferred_element_type=F32)
    scores_t = jax.nn.sigmoid(logits_t)
    sel, i1, i2, g1, g2 = _route_picks(scores_t + rb_ref[...], scores_t)
    first_lo = i1 < i2
    klo = jnp.where(first_lo, i1, i2)
    khi = jnp.where(first_lo, i2, i1)
    pair = jnp.where(klo == 0, 0, jnp.where(klo == 1, 3, 5)) + (khi - klo - 1)
    bucket_ref[...] = sel * PAIRS_PER_GROUP + pair
    tt_ref[...] = jnp.zeros_like(tt_ref)
    tt_ref[0:1, :] = jnp.where(first_lo, g1, g2)
    tt_ref[1:2, :] = jnp.where(first_lo, g2, g1)
    tt_ref[2:3, :] = jnp.full(g1.shape, 1.0, F32) * mod_of_tile(i).astype(F32)
    tail_ref[...] = tt_ref[...].T


def _moe_route(xs, streams, g, mod, layer, rw_t, rb_col, tm=1024):
    d = xs[0].shape[1]
    t = sum(st.rows for st in streams)
    n0 = streams[0].rows // tm
    s0, s1 = streams

    def mod_of_tile(i):
        row0 = s0.mod_base + s0.mod_step * (i * tm // s0.seq_len)
        row1 = s1.mod_base + s1.mod_step * ((i - n0) * tm // s1.seq_len)
        return jnp.where(i < n0, row0, row1)

    def mspec(chunk, st, shift):
        def imap(i):
            return (layer, chunk, st.mod_base + st.mod_step * (jnp.maximum(i - shift, 0) * tm // st.seq_len), 0, 0)
        return pl.BlockSpec((None, None, None, 1, d), imap)

    x_specs = _two_stream_specs(streams, tm, d)
    return pl.pallas_call(
        functools.partial(_moe_route_kernel, n0=n0, mod_of_tile=mod_of_tile),
        grid=(t // tm,),
        in_specs=[*x_specs,
                  pl.BlockSpec((1, d), lambda i: (0, 0)),
                  mspec(3, s0, 0), mspec(4, s0, 0), mspec(3, s1, n0), mspec(4, s1, n0),
                  pl.BlockSpec((LANES, d), lambda i: (0, 0)),
                  pl.BlockSpec((LANES, 1), lambda i: (0, 0))],
        out_specs=[pl.BlockSpec((tm, TAIL), lambda i: (i, 0)),
                   pl.BlockSpec((1, tm), lambda i: (0, i))],
        out_shape=[jax.ShapeDtypeStruct((t, TAIL), F32), jax.ShapeDtypeStruct((1, t), jnp.int32)],
        scratch_shapes=[pltpu.VMEM((LANES, tm), F32)],
        compiler_params=_params("parallel"),
        name="moe_route",
    )(*xs, g, mod, mod, mod, mod, rw_t, rb_col)


PLAN_E_LO, PLAN_E_HI, PLAN_N_TILES, PLAN_ROWS = 0, 1, 2, 8


def _moe_plan_kernel(bucket_ref, pos_ref, plan_ref, *, blk):
    n_tok = bucket_ref.shape[1]
    bid = lax.broadcasted_iota(jnp.int32, (BUCKET_ROWS, blk), 0)
    r = lax.broadcasted_iota(jnp.int32, (blk, blk), 0)
    c = lax.broadcasted_iota(jnp.int32, (blk, blk), 1)
    upper = (r <= c).astype(BF16)

    def onehot(k):
        return (bucket_ref[:, k * blk:(k + 1) * blk] == bid).astype(F32)

    counts = jnp.zeros((BUCKET_ROWS, 1), F32)
    for k in range(n_tok // blk):
        counts = counts + jnp.sum(onehot(k), axis=1, keepdims=True)
    padded = jnp.floor((counts + (MOE_TILE - 1)) * (1.0 / MOE_TILE)) * MOE_TILE
    padded_b = jnp.broadcast_to(padded, (BUCKET_ROWS, LANES))
    starts_b = _cumsum_rows(padded_b, False) - padded_b
    starts = starts_b[:, 0:1]

    carry = jnp.zeros((BUCKET_ROWS, 1), F32)
    for k in range(n_tok // blk):
        oh = onehot(k)
        incl = jnp.dot(oh.astype(BF16), upper, preferred_element_type=F32)
        slot = jnp.sum(oh * (starts + carry + incl - 1.0), axis=0, keepdims=True)
        pos_ref[:, k * blk:(k + 1) * blk] = slot.astype(jnp.int32)
        carry = carry + incl[:, blk - 1:blk]

    tile0 = (lax.broadcasted_iota(jnp.int32, (BUCKET_ROWS, LANES), 1) * MOE_TILE).astype(F32)
    inside = (tile0 >= starts_b) & (tile0 < starts_b + padded_b)
    brow = lax.broadcasted_iota(jnp.int32, (BUCKET_ROWS, LANES), 0).astype(F32)
    tb = jnp.sum(jnp.where(inside, brow, 0.0), axis=0, keepdims=True).astype(jnp.int32)
    n_tiles = jnp.sum(jnp.sum(inside.astype(F32), axis=0, keepdims=True),
                      axis=1, keepdims=True).astype(jnp.int32)
    one = jnp.ones_like(tb)
    grp = jnp.where(tb >= 3 * PAIRS_PER_GROUP, 3 * one,
                    jnp.where(tb >= 2 * PAIRS_PER_GROUP, 2 * one, jnp.where(tb >= PAIRS_PER_GROUP, one, 0 * one)))
    p = tb - grp * PAIRS_PER_GROUP
    klo = jnp.where(p >= 5, 2 * one, jnp.where(p >= 3, one, 0 * one))
    khi = p - jnp.where(klo == 0, 0, jnp.where(klo == 1, 3, 5)) + klo + 1
    plan_ref[...] = jnp.zeros_like(plan_ref)
    plan_ref[PLAN_E_LO:PLAN_E_LO + 1, :] = grp * EXPERTS_PER_GROUP + klo
    plan_ref[PLAN_E_HI:PLAN_E_HI + 1, :] = grp * EXPERTS_PER_GROUP + khi
    plan_ref[PLAN_N_TILES:PLAN_N_TILES + 1, :] = jnp.broadcast_to(n_tiles, (1, LANES))


def _moe_plan(bucket):
    t = bucket.shape[1]
    return pl.pallas_call(
        functools.partial(_moe_plan_kernel, blk=512),
        out_shape=[jax.ShapeDtypeStruct((1, t), jnp.int32),
                   jax.ShapeDtypeStruct((PLAN_ROWS, LANES), jnp.int32)],
        compiler_params=pltpu.CompilerParams(vmem_limit_bytes=VMEM_LIMIT),
        name="moe_plan",
    )(bucket)


def _max_tiles(t):
    return (t + N_BUCKETS * (MOE_TILE - 1)) // MOE_TILE


def _row_copy(src, src_row, dst, dst_row, sem):
    return pltpu.make_async_copy(src.at[pl.ds(src_row, 1)], dst.at[pl.ds(dst_row, 1)], sem)


def _moe_scatter_kernel(pos_ref, x0_ref, x1_ref, tail_ref, buf_ref, out_ref, aug_ref, sem, *, n0):
    del buf_ref
    i = pl.program_id(0)
    tm, d = x0_ref.shape
    aug_ref[:, :d] = jnp.where(i < n0, x0_ref[...], x1_ref[...])
    aug_ref[:, d:] = tail_ref[...]

    for r in range(tm):
        _row_copy(aug_ref, r, out_ref, pos_ref[0, r], sem).start()
    pltpu.make_async_copy(aug_ref, out_ref.at[pl.ds(0, tm)], sem).wait()


def _sorted_buffer(t, d):
    return jnp.zeros((_max_tiles(t) * MOE_TILE, d + TAIL), F32)


def _moe_scatter(xs, streams, tail, pos3, tm, buf):
    d = xs[0].shape[1]
    t = tail.shape[0]
    n0 = streams[0].rows // tm
    return pl.pallas_call(
        functools.partial(_moe_scatter_kernel, n0=n0),
        grid=(t // tm,),
        in_specs=[pl.BlockSpec((None, 1, tm), lambda i: (i, 0, 0), memory_space=pltpu.SMEM),
                  *_two_stream_specs(streams, tm, d),
                  pl.BlockSpec((tm, TAIL), lambda i: (i, 0)),
                  pl.BlockSpec(memory_space=pl.ANY)],
        out_specs=pl.BlockSpec(memory_space=pl.ANY),
        out_shape=jax.ShapeDtypeStruct(buf.shape, F32),
        input_output_aliases={4: 0},
        scratch_shapes=[pltpu.VMEM((tm, d + TAIL), F32), pltpu.SemaphoreType.DMA(())],
        compiler_params=_params("arbitrary"),
        name="moe_scatter",
    )(pos3, *xs, tail, buf)


def _moe_expert_kernel(plan_ref, xa_ref, g_ref, mt_ref, wg0_ref, wu0_ref, wd0_ref, wg1_ref, wu1_ref, wd1_ref,
                       o_ref, wgu_ref, wd_ref):
    i = pl.program_id(0)
    n_tiles = plan_ref[PLAN_N_TILES * LANES]
    d = o_ref.shape[1]
    f = wg0_ref.shape[1]

    cur = jnp.minimum(i, n_tiles - 1)
    prev = jnp.minimum(jnp.maximum(i - 1, 0), n_tiles - 1)
    changed = ((i == 0) | (plan_ref[PLAN_E_LO * LANES + cur] != plan_ref[PLAN_E_LO * LANES + prev])
               | (plan_ref[PLAN_E_HI * LANES + cur] != plan_ref[PLAN_E_HI * LANES + prev]))

    @pl.when(changed)
    def _():
        for k, (wg_ref, wu_ref, wdn_ref) in enumerate(((wg0_ref, wu0_ref, wd0_ref), (wg1_ref, wu1_ref, wd1_ref))):
            wgu_ref[k, :, :f] = wg_ref[...].astype(BF16)
            wgu_ref[k, :, f:] = wu_ref[...].astype(BF16)
            wd_ref[k] = wdn_ref[...].astype(BF16)

    @pl.when(i < n_tiles)
    def _():
        x = xa_ref[:, :d]
        gate_lo = xa_ref[:, d:d + 1]
        gate_hi = xa_ref[:, d + 1:d + 2]
        mrow = xa_ref[:, d + 2:d + 3]

        def pick(kind):
            base = 3 * kind
            return jnp.where(mrow < 0.5, mt_ref[base:base + 1, :],
                             jnp.where(mrow < 1.5, mt_ref[base + 1:base + 2, :], mt_ref[base + 2:base + 3, :]))

        h = _adaln(x, g_ref[...], pick(0), pick(1)).astype(BF16)
        out = None
        for k, gate in enumerate((gate_lo, gate_hi)):
            gu = jnp.dot(h, wgu_ref[k], preferred_element_type=F32)
            act = (_silu(gu[:, :f]) * gu[:, f:] * gate).astype(BF16)
            y = jnp.dot(act, wd_ref[k], preferred_element_type=F32)
            out = y if out is None else out + y
        o_ref[...] = x + pick(2) * out

    @pl.when(i >= n_tiles)
    def _():
        o_ref[...] = jnp.zeros_like(o_ref)


def _moe_experts(xa, plan, g, mtab, layer, wg, wu, wd):
    rows, da = xa.shape
    d = da - TAIL
    f = wg.shape[3]
    nt = rows // MOE_TILE

    def tile(i, plan_ref):
        return jnp.minimum(i, plan_ref[PLAN_N_TILES * LANES] - 1)

    def wspec(shape, row):
        return pl.BlockSpec((None, None, *shape), lambda i, p: (layer, p[row * LANES + tile(i, p)], 0, 0))

    grid_spec = pltpu.PrefetchScalarGridSpec(
        num_scalar_prefetch=1,
        grid=(nt,),
        in_specs=[pl.BlockSpec((MOE_TILE, da), lambda i, p: (tile(i, p), 0)),
                  pl.BlockSpec((1, d), lambda i, p: (0, 0)),
                  pl.BlockSpec(mtab.shape, lambda i, p: (0, 0)),
                  wspec((d, f), PLAN_E_LO), wspec((d, f), PLAN_E_LO), wspec((f, d), PLAN_E_LO),
                  wspec((d, f), PLAN_E_HI), wspec((d, f), PLAN_E_HI), wspec((f, d), PLAN_E_HI)],
        out_specs=pl.BlockSpec((MOE_TILE, d), lambda i, p: (i, 0)),
        scratch_shapes=[pltpu.VMEM((2, d, 2 * f), BF16), pltpu.VMEM((2, f, d), BF16)])
    return pl.pallas_call(
        _moe_expert_kernel,
        grid_spec=grid_spec,
        out_shape=jax.ShapeDtypeStruct((rows, d), F32),
        compiler_params=_params("arbitrary"),
        name="moe_experts",
    )(plan.reshape(-1), xa, g, mtab, wg, wu, wd, wg, wu, wd)


def _moe_gather_kernel(pos_ref, ys_ref, o_ref, sem):
    tm = o_ref.shape[0]

    for r in range(tm):
        _row_copy(ys_ref, pos_ref[0, r], o_ref, r, sem).start()
    pltpu.make_async_copy(ys_ref.at[pl.ds(0, tm)], o_ref, sem).wait()


def _moe_gather(ys, pos3, tile0, rows, tm):
    d = ys.shape[1]
    return pl.pallas_call(
        _moe_gather_kernel,
        grid=(rows // tm,),
        in_specs=[pl.BlockSpec((None, 1, tm), lambda i: (i + tile0, 0, 0), memory_space=pltpu.SMEM),
                  pl.BlockSpec(memory_space=pl.ANY)],
        out_specs=pl.BlockSpec((tm, d), lambda i: (i, 0)),
        out_shape=jax.ShapeDtypeStruct((rows, d), F32),
        scratch_shapes=[pltpu.SemaphoreType.DMA(())],
        compiler_params=_params("arbitrary"),
        name="moe_gather",
    )(pos3, ys)


def _moe(xs, streams, g, mod, layer, rw_t, rb_col, wg, wu, wd, buf, tm=1024):
    tail, bucket = _moe_route(xs, streams, g, mod, layer, rw_t, rb_col)
    pos, plan = _moe_plan(bucket)
    pos3 = pos.reshape(-1, 1, tm)
    xa = _moe_scatter(xs, streams, tail, pos3, tm, buf)
    mtab = mod[layer, 3:6, :, 0, :].reshape(-1, xs[0].shape[1])
    ys = _moe_experts(xa, plan, g, mtab, layer, wg, wu, wd)
    out, tile0 = [], 0
    for st in streams:
        out.append(_moe_gather(ys, pos3, tile0, st.rows, tm))
        tile0 += st.rows // tm
    return out, xa


def _rope_tables(t_len):
    rows = t_len // GRID_W
    row_ids = jnp.repeat(jnp.arange(rows), GRID_W).astype(F32)
    col_ids = jnp.tile(jnp.arange(GRID_W), rows).astype(F32)
    n_freq = HEAD_DIM // 4
    inv = 1.0 / (ROPE_THETA ** (jnp.arange(n_freq, dtype=F32) / n_freq))
    ang = jnp.concatenate([row_ids[:, None] * inv, col_ids[:, None] * inv], axis=-1)
    cos, sin = jnp.cos(ang), jnp.sin(ang)
    cs = jnp.tile(jnp.concatenate([cos, cos], axis=-1), (1, LANES // HEAD_DIM))
    sn = jnp.tile(jnp.concatenate([-sin, sin], axis=-1), (1, LANES // HEAD_DIM))
    return cs, sn


def _pad_lanes(v):
    return jnp.pad(v.reshape(1, -1), ((0, 0), (0, LANES - v.size)))


def kernel(x_prompt, x_sample, c, c_ctx, cache_k, cache_v, state_ssm, mod_w, mod_b, norm1_g, norm2_g,
           attn_w_qkv, attn_q_norm_g, attn_k_norm_g, attn_w_o, ssd_in_proj, ssd_conv_w, ssd_conv_b,
           ssd_dt_bias, ssd_a_log, ssd_d, ssd_norm_g, ssd_out_proj, router_w, router_b, expert_w_gate,
           expert_w_up, expert_w_down):
    n_p, l_p, d = x_prompt.shape
    n_s, l_s, _ = x_sample.shape
    depth = mod_w.shape[0]
    past = cache_k.shape[2]
    nkv = N_KV_HEADS * HEAD_DIM
    assert d == D_MODEL and 1 + n_s == 3
    streams = (_Stream(n_p, l_p, 0, 0), _Stream(n_s, l_s, 1, 1))
    xs = [x_prompt.reshape(n_p * l_p, d), x_sample.reshape(n_s * l_s, d)]

    cond8 = jnp.concatenate([c_ctx[None], c, jnp.zeros((8 - 1 - n_s, d), F32)], axis=0)
    mod = _modulation(cond8, mod_w, mod_b)
    mod = mod[:, :1 + n_s].reshape(depth, 1 + n_s, 6, 1, d).transpose(0, 2, 1, 3, 4)

    rope = _rope_tables(l_s)
    rw_t = jnp.pad(router_w.T, ((0, LANES - N_EXPERTS), (0, 0))).astype(BF16)
    rb_col = jnp.pad(router_b, (0, LANES - N_EXPERTS)).reshape(LANES, 1)

    sorted_buf = _sorted_buffer(sum(st.rows for st in streams), d)
    new_k, new_v, new_ssm = [], [], []
    for layer in range(depth):
        i = layer // 2
        g1 = norm1_g[layer][None]
        if layer % 2 == 0:
            w_qkv = attn_w_qkv[i].astype(BF16)
            w_o = attn_w_o[i].astype(BF16)
            qg2 = jnp.tile(attn_q_norm_g[i], LANES // HEAD_DIM)[None]
            kg2 = jnp.tile(attn_k_norm_g[i], LANES // HEAD_DIM)[None]
            for si, st in enumerate(streams):
                if si == 0:
                    q, k2, v2, kc, vc = _qk_prep(xs[si], g1, mod, layer, st, w_qkv, qg2, kg2, emit_cache=True)
                    o = _attention(q, k2, v2, st.n_seq, st.seq_len)
                    new_k.append(kc.reshape(n_p, l_p, N_KV_HEADS, HEAD_DIM))
                    new_v.append(vc.reshape(n_p, l_p, N_KV_HEADS, HEAD_DIM))
                else:
                    q, k2, v2 = _qk_prep(xs[si], g1, mod, layer, st, w_qkv, qg2, kg2, tables=rope)
                    kc2, vc2 = _kv_expand(cache_k[:, i].reshape(n_s * past, nkv),
                                          cache_v[:, i].reshape(n_s * past, nkv))
                    o = _attention(q, k2, v2, st.n_seq, st.seq_len, ctx=(kc2, vc2, past))
                xs[si] = _mm_res(o, w_o, xs[si], mod, layer, st)
        else:
            w_in = ssd_in_proj[i]
            nzx = D_INNER + CONV_DIM
            w_zx = w_in[:, :nzx].astype(BF16)
            w_dt = jnp.pad(w_in[:, nzx:], ((0, 0), (0, LANES - 2 * SSD_HEADS))).astype(BF16)
            w_out = ssd_out_proj[i].astype(BF16)
            dtb = _pad_lanes(ssd_dt_bias[i])
            alog = _pad_lanes(ssd_a_log[i])
            dskip = jnp.repeat(ssd_d[i], SSD_HEAD_DIM)[None]
            cw, cb = ssd_conv_w[i], ssd_conv_b[i][None]
            for si, st in enumerate(streams):
                z, xbc, dt = _in_proj_conv(xs[si], g1, mod, layer, st, w_zx, w_dt, cw, cb)
                init = None if si == 0 else state_ssm[:, i].reshape(n_s, 2, D_INNER, D_STATE)
                y_f, s_f = _ssd_scan(xbc, dt, dtb, alog, st, rev=False, init=init)
                yg, s_b = _ssd_scan(xbc, dt, dtb, alog, st, rev=True, init=init, final=(y_f, z, dskip))
                if si == 0:
                    new_ssm.append(jnp.stack([s_f, s_b], axis=1).reshape(
                        n_p, 2, SSD_HEADS, SSD_HEAD_DIM, D_STATE))
                xs[si] = _mm_res(yg, w_out, xs[si], mod, layer, st, norm_g=ssd_norm_g[i][None])
        xs, sorted_buf = _moe(xs, streams, norm2_g[layer][None], mod, layer, rw_t, rb_col,
                              expert_w_gate, expert_w_up, expert_w_down, sorted_buf)

    return (xs[0].reshape(n_p, l_p, d), xs[1].reshape(n_s, l_s, d),
            jnp.stack(new_k, axis=1), jnp.stack(new_v, axis=1), jnp.stack(new_ssm, axis=1))
```
